```python
import math
import jax, jax.numpy as jnp
from jax import lax
import numpy as np

D_MODEL = 2048
BATCH = 4
SEQ = 2048
DEPTH = 1
DEC_BATCH = 128
DEC_SEQ = 4
PAST_LEN = 2048
PAGE_SIZE = 128

N_HEADS = 8
HEAD_DIM = 128
V_DIM = 2 * HEAD_DIM
QK_WIDTH = N_HEADS * 2 * HEAD_DIM
ATTN_WIDTH = N_HEADS * V_DIM
POOL_WINDOWS = (2, 4, 8, 16)
N_POOL_GROUPS = len(POOL_WINDOWS)
POOL_WIDTH = D_MODEL // 2
POOL_GROUP = POOL_WIDTH // N_POOL_GROUPS
POOL_BUF = max(POOL_WINDOWS) - 1
ROPE_THETA = 10000.0
EPS = 1e-6
Q_BLOCK = 128
LAMBDA_STD = 0.1
IN_WIDTHS = (POOL_WIDTH, POOL_WIDTH, QK_WIDTH, QK_WIDTH, ATTN_WIDTH, ATTN_WIDTH, D_MODEL, D_MODEL)
IN_WIDTH = sum(IN_WIDTHS)
IN_SPLITS = tuple(int(s) for s in np.cumsum(IN_WIDTHS)[:-1])

kernel_name = 'pool_diffattn_hybrid_decode_step'


def rms_norm(x, g):
    xf = x.astype(jnp.float32)
    y = xf * lax.rsqrt(jnp.mean(xf * xf, axis=-1, keepdims=True) + EPS)
    return (y * g.astype(jnp.float32)).astype(x.dtype)


def rope(x, pos):
    half = HEAD_DIM // 2
    inv_freq = ROPE_THETA ** (-jnp.arange(half, dtype=jnp.float32) / half)
    ang = pos.astype(jnp.float32)[:, None] * inv_freq[None, :]
    cos = jnp.cos(ang)[None, :, None, None, :]
    sin = jnp.sin(ang)[None, :, None, None, :]
    xf = x.astype(jnp.float32)
    x1, x2 = xf[..., :half], xf[..., half:]
    return jnp.concatenate([x1 * cos - x2 * sin, x2 * cos + x1 * sin], axis=-1).astype(x.dtype)


def pool_mixer(u_ext, n_new, w_grp, scale):
    B, L, _ = u_ext.shape
    uf = u_ext.astype(jnp.float32)
    csum = jnp.concatenate([jnp.zeros_like(uf[:, :1]), jnp.cumsum(uf, axis=1)], axis=1)
    end = jnp.arange(L - n_new, L) + 1
    groups = []
    for g, w in enumerate(POOL_WINDOWS):
        sl = slice(g * POOL_GROUP, (g + 1) * POOL_GROUP)
        start = jnp.maximum(end - w, 0)
        cnt = (end - start).astype(jnp.float32)[None, :, None]
        mean = (csum[:, end, sl] - csum[:, start, sl]) / cnt
        groups.append(mean - uf[:, L - n_new:, sl])
    pooled = jnp.stack(groups, axis=2).astype(u_ext.dtype)
    mixed = jnp.einsum('btgc,gcd->btgd', pooled, w_grp)
    return mixed.reshape(B, n_new, POOL_WIDTH) * scale


def diff_core(q, k, v, mask, lam):
    s = jnp.einsum('bqhmd,bkhmd->bhmqk', q, k, preferred_element_type=jnp.float32) * (HEAD_DIM ** -0.5)
    s = jnp.where(mask, s, -jnp.inf)
    a = jax.nn.softmax(s, axis=-1)
    diff = a[:, :, 0] - lam * a[:, :, 1]
    return jnp.einsum('bhqk,bkhv->bqhv', diff.astype(v.dtype), v)


def diff_attn_prompt(q, k, v, lam):
    B, S = q.shape[0], q.shape[1]
    nb = S // Q_BLOCK
    qb = q.reshape(B, nb, Q_BLOCK, N_HEADS, 2, HEAD_DIM).transpose(1, 0, 2, 3, 4, 5)
    kpos = jnp.arange(S)

    def block(args):
        qi, i = args
        qpos = i * Q_BLOCK + jnp.arange(Q_BLOCK)
        mask = kpos[None, :] <= qpos[:, None]
        return diff_core(qi, k, v, mask, lam)

    out = lax.map(block, (qb, jnp.arange(nb)))
    return out.transpose(1, 0, 2, 3, 4).reshape(B, S, N_HEADS, V_DIM)


def mixer_layer(x, c, pos, k_past, v_past, u_past, lam_init,
                w_ada, b_ada, g_pre, g_post, w_in, w_pool_grp, pool_scale,
                lq1, lk1, lq2, lk2, g_subln, w_proj_pool, w_proj_attn, w_out):
    B, T, _ = x.shape
    mod = jax.nn.silu(c) @ w_ada + b_ada
    shift, scale, gate = jnp.split(mod, 3, axis=-1)
    h = rms_norm(x, g_pre) * (1 + scale[:, None, :]) + shift[:, None, :]
    proj = h @ w_in
    u, z_pool, q, k, v, z_attn, gl_pool, gl_attn = jnp.split(proj, IN_SPLITS, axis=-1)

    u_ext = u if u_past is None else jnp.concatenate([u_past.astype(u.dtype), u], axis=1)
    pool_out = pool_mixer(u_ext, T, w_pool_grp, pool_scale) * jax.nn.silu(z_pool)
    u_new = u_ext[:, -POOL_BUF:]

    q = rope(q.reshape(B, T, N_HEADS, 2, HEAD_DIM), pos)
    k = rope(k.reshape(B, T, N_HEADS, 2, HEAD_DIM), pos)
    v = v.reshape(B, T, N_HEADS, V_DIM)
    f32 = jnp.float32
    lam = (jnp.exp(jnp.sum(lq1.astype(f32) * lk1.astype(f32)))
           - jnp.exp(jnp.sum(lq2.astype(f32) * lk2.astype(f32))) + lam_init)
    if k_past is None:
        o = diff_attn_prompt(q, k, v, lam)
    else:
        P = k_past.shape[1]
        k_all = jnp.concatenate([k_past.astype(k.dtype), k], axis=1)
        v_all = jnp.concatenate([v_past.astype(v.dtype), v], axis=1)
        mask = jnp.arange(P + T)[None, :] <= (P + jnp.arange(T))[:, None]
        o = diff_core(q, k_all, v_all, mask, lam)
    o = rms_norm(o, g_subln) * (1 - lam_init)
    attn_out = o.reshape(B, T, ATTN_WIDTH) * jax.nn.silu(z_attn)

    merged = (jax.nn.sigmoid(gl_pool) * (pool_out @ w_proj_pool)
              + jax.nn.sigmoid(gl_attn) * (attn_out @ w_proj_attn))
    out = merged @ w_out
    y = x + gate[:, None, :] * rms_norm(out, g_post)
    return y, k, v, u_new


def setup_inputs(seed: int = 0) -> dict:
    key = jax.random.key(seed)
    ks = jax.random.split(key, 32)
    f32 = jnp.float32
    n_pages = PAST_LEN // PAGE_SIZE
    n_used = DEC_BATCH * n_pages
    n_pool_pages = (n_used * 5) // 4
    nrm = lambda k, shape, s=1.0: jax.random.normal(k, shape, f32) * s
    page_table = jax.random.permutation(ks[0], n_pool_pages)[:n_used].reshape(DEC_BATCH, n_pages).astype(jnp.int32)
    return {
        'x_prompt': nrm(ks[1], (BATCH, SEQ, D_MODEL)),
        'x_sample': nrm(ks[2], (DEC_BATCH, DEC_SEQ, D_MODEL)),
        'cache_k': nrm(ks[3], (DEPTH, n_pool_pages, PAGE_SIZE, N_HEADS, 2, HEAD_DIM)),
        'cache_v': nrm(ks[4], (DEPTH, n_pool_pages, PAGE_SIZE, N_HEADS, V_DIM)),
        'state_pool': nrm(ks[5], (DEPTH, DEC_BATCH, POOL_BUF, POOL_WIDTH)),
        'page_table': page_table,
        'c_prompt': nrm(ks[6], (BATCH, D_MODEL)),
        'c_sample': nrm(ks[7], (DEC_BATCH, D_MODEL)),
        'w_ada': nrm(ks[8], (DEPTH, D_MODEL, 3 * D_MODEL), 0.5 * D_MODEL ** -0.5),
        'b_ada': nrm(ks[9], (DEPTH, 3 * D_MODEL), 0.02),
        'g_pre': 1.0 + nrm(ks[10], (DEPTH, D_MODEL), 0.1),
        'g_post': 1.0 + nrm(ks[11], (DEPTH, D_MODEL), 0.1),
        'w_in': nrm(ks[12], (DEPTH, D_MODEL, IN_WIDTH), D_MODEL ** -0.5),
        'w_pool_grp': nrm(ks[13], (DEPTH, N_POOL_GROUPS, POOL_GROUP, POOL_GROUP), POOL_GROUP ** -0.5),
        'pool_scale': 1.0 + nrm(ks[14], (DEPTH, POOL_WIDTH), 0.1),
        'lambda_q1': nrm(ks[15], (DEPTH, HEAD_DIM), LAMBDA_STD),
        'lambda_k1': nrm(ks[16], (DEPTH, HEAD_DIM), LAMBDA_STD),
        'lambda_q2': nrm(ks[17], (DEPTH, HEAD_DIM), LAMBDA_STD),
        'lambda_k2': nrm(ks[18], (DEPTH, HEAD_DIM), LAMBDA_STD),
        'g_subln': 1.0 + nrm(ks[19], (DEPTH, V_DIM), 0.1),
        'w_proj_pool': nrm(ks[20], (DEPTH, POOL_WIDTH, D_MODEL), POOL_WIDTH ** -0.5),
        'w_proj_attn': nrm(ks[21], (DEPTH, ATTN_WIDTH, D_MODEL), ATTN_WIDTH ** -0.5),
        'w_out': nrm(ks[22], (DEPTH, D_MODEL, D_MODEL), D_MODEL ** -0.5),
    }


def reference(x_prompt, x_sample, cache_k, cache_v, state_pool, page_table, c_prompt, c_sample,
              w_ada, b_ada, g_pre, g_post, w_in, w_pool_grp, pool_scale,
              lambda_q1, lambda_k1, lambda_q2, lambda_k2, g_subln, w_proj_pool, w_proj_attn, w_out):
    dec_b, n_pages = page_table.shape
    page_size = cache_k.shape[2]
    past_len = n_pages * page_size
    pos_prompt = jnp.arange(x_prompt.shape[1])
    pos_sample = past_len + jnp.arange(x_sample.shape[1])
    yp, ys = x_prompt, x_sample
    kp_l, vp_l, up_l, ks_l, vs_l, us_l = [], [], [], [], [], []
    for l in range(DEPTH):
        lam_init = 0.8 - 0.6 * math.exp(-0.3 * l)
        w = (w_ada[l], b_ada[l], g_pre[l], g_post[l], w_in[l], w_pool_grp[l], pool_scale[l],
             lambda_q1[l], lambda_k1[l], lambda_q2[l], lambda_k2[l], g_subln[l],
             w_proj_pool[l], w_proj_attn[l], w_out[l])
        yp, kp, vp, up = mixer_layer(yp, c_prompt, pos_prompt, None, None, None, lam_init, *w)
        k_past = cache_k[l][page_table].reshape(dec_b, past_len, N_HEADS, 2, HEAD_DIM)
        v_past = cache_v[l][page_table].reshape(dec_b, past_len, N_HEADS, V_DIM)
        ys, ksn, vsn, usn = mixer_layer(ys, c_sample, pos_sample, k_past, v_past, state_pool[l], lam_init, *w)
        kp_l.append(kp); vp_l.append(vp); up_l.append(up)
        ks_l.append(ksn); vs_l.append(vsn); us_l.append(usn)
    return (yp, ys, jnp.stack(kp_l), jnp.stack(vp_l), jnp.stack(up_l),
            jnp.stack(ks_l), jnp.stack(vs_l), jnp.stack(us_l))
```

```python
import functools
import math

import jax
import jax.numpy as jnp
from jax import lax
from jax.experimental import pallas as pl
from jax.experimental.pallas import tpu as pltpu

F32 = jnp.float32
BF16 = jnp.bfloat16

EPS = 1e-6
ROPE_THETA = 10000.0
N_HEADS = 8
HEAD_DIM = 128
V_DIM = 2 * HEAD_DIM
D_MODEL = 2048
POOL_WIDTH = 1024
POOL_GROUP = 256
POOL_WINDOWS = (2, 4, 8, 16)
POOL_BUF = 15
LANES = 128
HALO = 16
AUX_WIDTH = 8192
VMEM_LIMIT = 56 * 1024 * 1024


def _params(sem):
    return pltpu.CompilerParams(dimension_semantics=sem, vmem_limit_bytes=VMEM_LIMIT)


def _dot(a, b):
    return jnp.dot(a, b, preferred_element_type=F32)


def _dot_nt(a, b):
    return lax.dot_general(a, b, (((1,), (1,)), ((), ())), preferred_element_type=F32)


def _sigmoid(x):
    return jax.nn.sigmoid(x)


def _mod_kernel(c_ref, w_ref, b_ref, o_ref):
    c = c_ref[...]
    a = (c * _sigmoid(c)).astype(BF16)
    o_ref[...] = _dot(a, w_ref[...].astype(BF16)) + b_ref[...]


def _mod(c_all, w_ada, b_ada, tn=512):
    m, d = c_all.shape
    n = w_ada.shape[1]
    return pl.pallas_call(
        _mod_kernel,
        grid=(n // tn,),
        in_specs=[pl.BlockSpec((m, d), lambda j: (0, 0)),
                  pl.BlockSpec((d, tn), lambda j: (0, j)),
                  pl.BlockSpec((1, tn), lambda j: (0, j))],
        out_specs=pl.BlockSpec((m, tn), lambda j: (0, j)),
        out_shape=jax.ShapeDtypeStruct((m, n), F32),
        compiler_params=_params(("parallel",)),
        name="adaln_mod",
    )(c_all, w_ada, b_ada.reshape(1, n))


def _h_kernel(x_ref, g_ref, sc_ref, sh_ref, o_ref):
    x = x_ref[...]
    ms = jnp.mean(x * x, axis=-1, keepdims=True)
    y = x * lax.rsqrt(ms + EPS) * g_ref[...]
    o_ref[...] = (y * (1.0 + sc_ref[...]) + sh_ref[...]).astype(BF16)


def _modulated_norm(x2d, g_pre, mod, mod_spec, tm):
    m, d = x2d.shape
    return pl.pallas_call(
        _h_kernel,
        grid=(m // tm,),
        in_specs=[pl.BlockSpec((tm, d), lambda i: (i, 0)),
                  pl.BlockSpec((1, d), lambda i: (0, 0)),
                  mod_spec(1), mod_spec(0)],
        out_specs=pl.BlockSpec((tm, d), lambda i: (i, 0)),
        out_shape=jax.ShapeDtypeStruct((m, d), BF16),
        compiler_params=_params(("parallel",)),
        name="modulated_norm",
    )(x2d, g_pre.reshape(1, d), mod, mod)


def _aux_kernel(h_ref, w_ref, o_ref, *, tn):
    j = pl.program_id(1)
    acc = _dot(h_ref[...], w_ref[...].astype(BF16))
    n_plain = POOL_WIDTH // tn
    n_silu_end = (2 * POOL_WIDTH + D_MODEL) // tn

    @pl.when(j < n_plain)
    def _():
        o_ref[...] = acc

    @pl.when((j >= n_plain) & (j < n_silu_end))
    def _():
        o_ref[...] = acc * _sigmoid(acc)

    @pl.when(j >= n_silu_end)
    def _():
        o_ref[...] = _sigmoid(acc)


def _aux_proj(h, w_in, tm, tn=512):
    m, d = h.shape
    n_head = (2 * POOL_WIDTH) // tn
    skip = (3 * D_MODEL) // tn
    return pl.pallas_call(
        functools.partial(_aux_kernel, tn=tn),
        grid=(m // tm, AUX_WIDTH // tn),
        in_specs=[pl.BlockSpec((tm, d), lambda i, j: (i, 0)),
                  pl.BlockSpec((d, tn), lambda i, j: (0, jnp.where(j < n_head, j, j + skip)))],
        out_specs=pl.BlockSpec((tm, tn), lambda i, j: (i, j)),
        out_shape=jax.ShapeDtypeStruct((m, AUX_WIDTH), F32),
        compiler_params=_params(("parallel", "arbitrary")),
        name="aux_proj",
    )(h, w_in)


def _rope(acc, cos, sin_signed):
    parts = []
    for c in range(acc.shape[1] // LANES):
        xs = acc[:, c * LANES:(c + 1) * LANES]
        parts.append(xs * cos + pltpu.roll(xs, HEAD_DIM // 2, axis=1) * sin_signed)
    return jnp.concatenate(parts, axis=1)


def _qkv_kernel(h_ref, w_ref, cos_ref, sin_ref, lo_ref, k_ref, v_ref, *, tn):
    j = pl.program_id(1)
    nb = D_MODEL // tn
    acc = _dot(h_ref[...], w_ref[...].astype(BF16))

    @pl.when(j < nb)
    def _():
        lo_ref[...] = _rope(acc, cos_ref[...], sin_ref[...]).astype(BF16).astype(lo_ref.dtype)

    @pl.when((j >= nb) & (j < 2 * nb))
    def _():
        r = _rope(acc, cos_ref[...], sin_ref[...])
        k_ref[...] = r
        lo_ref[...] = r.astype(BF16).astype(lo_ref.dtype)

    @pl.when(j >= 2 * nb)
    def _():
        v_ref[...] = acc
        lo_ref[...] = acc.astype(BF16).astype(lo_ref.dtype)


def _qkv_proj(h, w_in, cos2, sin2, tm, lo_dtype, tn=512):
    m, d = h.shape
    nb = D_MODEL // tn
    col0 = (2 * POOL_WIDTH) // tn
    n_pos = cos2.shape[0] // tm
    return pl.pallas_call(
        functools.partial(_qkv_kernel, tn=tn),
        grid=(m // tm, 3 * nb),
        in_specs=[pl.BlockSpec((tm, d), lambda i, j: (i, 0)),
                  pl.BlockSpec((d, tn), lambda i, j: (0, col0 + j)),
                  pl.BlockSpec((tm, LANES), lambda i, j: (i % n_pos, 0)),
                  pl.BlockSpec((tm, LANES), lambda i, j: (i % n_pos, 0))],
        out_specs=[pl.BlockSpec((tm, tn), lambda i, j: (i, j)),
                   pl.BlockSpec((tm, tn), lambda i, j: (i, jnp.clip(j - nb, 0, nb - 1))),
                   pl.BlockSpec((tm, tn), lambda i, j: (i, jnp.clip(j - 2 * nb, 0, nb - 1)))],
        out_shape=[jax.ShapeDtypeStruct((m, 3 * D_MODEL), lo_dtype),
                   jax.ShapeDtypeStruct((m, D_MODEL), F32),
                   jax.ShapeDtypeStruct((m, D_MODEL), F32)],
        compiler_params=_params(("parallel", "arbitrary")),
        name="qkv_proj",
    )(h, w_in, cos2, sin2)


def _pool_mix(pooled_groups, wg_ref, scale_ref, szp):
    mixed = [_dot(p.astype(BF16), wg_ref[g].astype(BF16)) for g, p in enumerate(pooled_groups)]
    mixed = jnp.concatenate(mixed, axis=1)
    return (mixed * scale_ref[...] * szp).astype(BF16)


def _pool_prompt_kernel(u_ref, halo_ref, szp_ref, wg_ref, scale_ref, o_ref, *, tp, tiles_per_seq):
    i = pl.program_id(0)
    first = (i % tiles_per_seq) == 0
    halo = jnp.where(first, 0.0, halo_ref[...])
    ext = jnp.concatenate([halo, u_ref[...]], axis=0)
    pos = (i % tiles_per_seq) * tp + lax.broadcasted_iota(jnp.int32, (tp, 1), 0)
    groups = []
    for g, w in enumerate(POOL_WINDOWS):
        e = ext[:, g * POOL_GROUP:(g + 1) * POOL_GROUP]
        a = e
        s = 1
        while s < w:
            a = a + pltpu.roll(a, s, axis=0)
            s *= 2
        cnt = jnp.minimum(pos + 1, w).astype(F32)
        groups.append(a[HALO:] / cnt - e[HALO:])
    o_ref[...] = _pool_mix(groups, wg_ref, scale_ref, szp_ref[...])


def _pool_prompt(aux, w_pool_grp, pool_scale, seq, tp=512):
    m = aux.shape[0]
    tiles_per_seq = seq // tp
    hb = tp // HALO
    return pl.pallas_call(
        functools.partial(_pool_prompt_kernel, tp=tp, tiles_per_seq=tiles_per_seq),
        grid=(m // tp,),
        in_specs=[pl.BlockSpec((tp, POOL_WIDTH), lambda i: (i, 0)),
                  pl.BlockSpec((HALO, POOL_WIDTH), lambda i: (jnp.maximum(i * hb - 1, 0), 0)),
                  pl.BlockSpec((tp, POOL_WIDTH), lambda i: (i, 1)),
                  pl.BlockSpec((4, POOL_GROUP, POOL_GROUP), lambda i: (0, 0, 0)),
                  pl.BlockSpec((1, POOL_WIDTH), lambda i: (0, 0))],
        out_specs=pl.BlockSpec((tp, POOL_WIDTH), lambda i: (i, 0)),
        out_shape=jax.ShapeDtypeStruct((m, POOL_WIDTH), BF16),
        compiler_params=_params(("parallel",)),
        name="pool_prompt",
    )(aux, aux, aux, w_pool_grp, pool_scale.reshape(1, POOL_WIDTH))


def _pool_sample_kernel(sp_ref, u_ref, szp_ref, wg_ref, scale_ref, o_ref, *, n_new):
    nb = u_ref.shape[1]
    ext = [sp_ref[r] for r in range(POOL_BUF)] + [u_ref[t] for t in range(n_new)]
    groups = []
    for g, w in enumerate(POOL_WINDOWS):
        per_t = []
        for t in range(n_new):
            end = POOL_BUF + t
            acc = ext[end][:, g * POOL_GROUP:(g + 1) * POOL_GROUP]
            cur = acc
            for r in range(end - w + 1, end):
                acc = acc + ext[r][:, g * POOL_GROUP:(g + 1) * POOL_GROUP]
            per_t.append(acc / float(w) - cur)
        groups.append(jnp.concatenate(per_t, axis=0))
    szp = szp_ref[...].reshape(n_new * nb, POOL_WIDTH)
    o_ref[...] = _pool_mix(groups, wg_ref, scale_ref, szp)


def _pool_sample(sp_t, aux_s3, w_pool_grp, pool_scale):
    n_new, nb, _ = aux_s3.shape
    return pl.pallas_call(
        functools.partial(_pool_sample_kernel, n_new=n_new),
        grid=(1,),
        in_specs=[pl.BlockSpec((POOL_BUF, nb, POOL_WIDTH), lambda i: (0, 0, 0)),
                  pl.BlockSpec((n_new, nb, POOL_WIDTH), lambda i: (0, 0, 0)),
                  pl.BlockSpec((n_new, nb, POOL_WIDTH), lambda i: (0, 0, 1)),
                  pl.BlockSpec((4, POOL_GROUP, POOL_GROUP), lambda i: (0, 0, 0)),
                  pl.BlockSpec((1, POOL_WIDTH), lambda i: (0, 0))],
        out_specs=pl.BlockSpec((n_new * nb, POOL_WIDTH), lambda i: (0, 0)),
        out_shape=jax.ShapeDtypeStruct((n_new * nb, POOL_WIDTH), BF16),
        compiler_params=_params(("arbitrary",)),
        name="pool_sample",
    )(sp_t, aux_s3, aux_s3, w_pool_grp, pool_scale.reshape(1, POOL_WIDTH))


def _lambda(lv, lam_init):
    l1 = jnp.sum(lv[0:1] * lv[1:2], axis=1, keepdims=True)
    l2 = jnp.sum(lv[2:3] * lv[3:4], axis=1, keepdims=True)
    return jnp.exp(l1) - jnp.exp(l2) + lam_init


def _subln_gate(o, g, sz, lam_init):
    ms = jnp.mean(o * o, axis=-1, keepdims=True)
    y = o * lax.rsqrt(ms + EPS) * g
    return (y * (1.0 - lam_init) * sz).astype(BF16)


def _attn_prompt_kernel(lv_ref, q_ref, k_ref, v_ref, sz_ref, g_ref, o_ref,
                        s_ref, m_ref, l_ref, acc_ref, *, tq, lam_init, scale):
    qi = pl.program_id(2)
    lam = _lambda(lv_ref[...], lam_init)
    q = (q_ref[:, :HEAD_DIM], q_ref[:, HEAD_DIM:])
    neg = jnp.full((tq, LANES), -jnp.inf, F32)
    m_ref[0] = neg
    m_ref[1] = neg

    def fold(x, op):
        r = x[:, :LANES]
        for c in range(1, tq // LANES):
            r = op(r, x[:, c * LANES:(c + 1) * LANES])
        return r

    def score_block(kc, masked):
        off = pl.multiple_of(kc * tq, tq)
        kblk = k_ref[pl.ds(off, tq), :]
        for mp in range(2):
            s = _dot_nt(q[mp], kblk[:, mp * HEAD_DIM:(mp + 1) * HEAD_DIM]) * scale
            if masked:
                row = lax.broadcasted_iota(jnp.int32, (tq, tq), 0)
                col = lax.broadcasted_iota(jnp.int32, (tq, tq), 1)
                s = jnp.where(col <= row, s, -jnp.inf)
            s_ref[mp, kc] = s
            m_ref[mp] = jnp.maximum(m_ref[mp], fold(s, jnp.maximum))

    def body_a(kc, carry):
        score_block(kc, False)
        return carry

    lax.fori_loop(0, qi, body_a, 0)
    score_block(qi, True)

    mx = [jnp.max(m_ref[mp], axis=1, keepdims=True) for mp in range(2)]
    zero = jnp.zeros((tq, LANES), F32)
    l_ref[0] = zero
    l_ref[1] = zero

    def body_b(kc, carry):
        for mp in range(2):
            p = jnp.exp(s_ref[mp, kc] - mx[mp])
            s_ref[mp, kc] = p
            l_ref[mp] = l_ref[mp] + fold(p, jnp.add)
        return carry

    lax.fori_loop(0, qi + 1, body_b, 0)

    r1 = 1.0 / jnp.sum(l_ref[0], axis=1, keepdims=True)
    r2 = lam / jnp.sum(l_ref[1], axis=1, keepdims=True)
    acc_ref[...] = jnp.zeros_like(acc_ref)

    def body_c(kc, carry):
        off = pl.multiple_of(kc * tq, tq)
        d = (s_ref[0, kc] * r1 - s_ref[1, kc] * r2).astype(BF16)
        acc_ref[...] += _dot(d, v_ref[pl.ds(off, tq), :])
        return carry

    lax.fori_loop(0, qi + 1, body_c, 0)
    o_ref[...] = _subln_gate(acc_ref[...], g_ref[...], sz_ref[...], lam_init)


def _attn_prompt(lam_vecs, qkv16, aux, g_subln, batch, seq, lam_init, tq=256):
    m = qkv16.shape[0]
    nq = seq // tq
    sza0 = (2 * POOL_WIDTH) // V_DIM
    kern = functools.partial(_attn_prompt_kernel, tq=tq, lam_init=lam_init, scale=HEAD_DIM ** -0.5)
    return pl.pallas_call(
        kern,
        grid=(batch, N_HEADS, nq),
        in_specs=[pl.BlockSpec((4, HEAD_DIM), lambda b, h, qi: (0, 0)),
                  pl.BlockSpec((tq, V_DIM), lambda b, h, qi: (b * nq + qi, h)),
                  pl.BlockSpec((seq, V_DIM), lambda b, h, qi: (b, N_HEADS + h)),
                  pl.BlockSpec((seq, V_DIM), lambda b, h, qi: (b, 2 * N_HEADS + h)),
                  pl.BlockSpec((tq, V_DIM), lambda b, h, qi: (b * nq + qi, sza0 + h)),
                  pl.BlockSpec((1, V_DIM), lambda b, h, qi: (0, 0))],
        out_specs=pl.BlockSpec((tq, V_DIM), lambda b, h, qi: (b * nq + qi, h)),
        out_shape=jax.ShapeDtypeStruct((m, N_HEADS * V_DIM), BF16),
        scratch_shapes=[pltpu.VMEM((2, nq, tq, tq), F32),
                        pltpu.VMEM((2, tq, LANES), F32),
                        pltpu.VMEM((2, tq, LANES), F32),
                        pltpu.VMEM((tq, V_DIM), F32)],
        compiler_params=_params(("parallel", "parallel", "arbitrary")),
        name="attn_prompt",
    )(lam_vecs, qkv16, qkv16, qkv16, aux, g_subln.reshape(1, V_DIM))


def _attn_decode_kernel(pt_ref, lv_ref, q_ref, kn_ref, vn_ref, sz_ref, g_ref, *rest,
                        pps, nchunk, n_new, lam_init, scale):
    del pt_ref
    k_refs = rest[:pps]
    v_refs = rest[pps:2 * pps]
    o_ref = rest[2 * pps]
    s_ref, p_ref, pn_ref, acc_ref = rest[2 * pps + 1:]
    j = pl.program_id(1)
    nrow = n_new * N_HEADS
    ncol = k_refs[0].shape[0] // 2

    @pl.when(j < nchunk)
    def _():
        qq = q_ref[...].reshape(nrow, V_DIM)
        rows = lax.broadcasted_iota(jnp.int32, (nrow, ncol), 0)
        cols = lax.broadcasted_iota(jnp.int32, (nrow, ncol), 1)
        valid = (cols & (N_HEADS - 1)) == (rows & (N_HEADS - 1))
        for mp in range(2):
            qm = qq[:, mp * HEAD_DIM:(mp + 1) * HEAD_DIM].astype(BF16)
            for i in range(pps):
                km = k_refs[i][pl.ds(mp, ncol, stride=2), :].astype(BF16)
                s = _dot_nt(qm, km) * scale
                s_ref[j, mp, :, i * ncol:(i + 1) * ncol] = jnp.where(valid, s, -jnp.inf)

    @pl.when(j == nchunk - 1)
    def _():
        lam = _lambda(lv_ref[...], lam_init)
        qq = q_ref[...].reshape(nrow, V_DIM)
        lane = lax.broadcasted_iota(jnp.int32, (nrow, LANES), 1)
        qrow = lax.broadcasted_iota(jnp.int32, (nrow, LANES), 0) >> 3
        sn = [jnp.full((nrow, LANES), -jnp.inf, F32) for _ in range(2)]
        for t in range(n_new):
            ktile = jnp.concatenate([kn_ref[t]] * n_new, axis=0)
            prod = qq * ktile
            for mp in range(2):
                col = jnp.sum(prod[:, mp * HEAD_DIM:(mp + 1) * HEAD_DIM], axis=1, keepdims=True) * scale
                sn[mp] = jnp.where((lane == t) & (qrow >= t), col, sn[mp])
        pn = []
        rinv = []
        for mp in range(2):
            mx = jnp.max(sn[mp], axis=1, keepdims=True)
            for c in range(nchunk):
                mx = jnp.maximum(mx, jnp.max(s_ref[c, mp], axis=1, keepdims=True))
            pnew = jnp.exp(sn[mp] - mx)
            l = jnp.sum(pnew, axis=1, keepdims=True)
            for c in range(nchunk):
                p = jnp.exp(s_ref[c, mp] - mx)
                s_ref[c, mp] = p
                l = l + jnp.sum(p, axis=1, keepdims=True)
            pn.append(pnew)
            rinv.append(1.0 / l)
        r1 = rinv[0]
        r2 = lam * rinv[1]
        for c in range(nchunk):
            p_ref[c] = (s_ref[c, 0] * r1 - s_ref[c, 1] * r2).astype(BF16)
        pn_ref[...] = (pn[0] * r1 - pn[1] * r2).astype(BF16).astype(F32)

    @pl.when(j == nchunk)
    def _():
        acc_ref[...] = jnp.zeros_like(acc_ref)

    @pl.when(j >= nchunk)
    def _():
        c = j - nchunk
        a = None
        for i in range(pps):
            vp = v_refs[i][...].astype(BF16)
            d = _dot(p_ref[c, :, i * ncol:(i + 1) * ncol], vp)
            a = d if a is None else a + d
        acc_ref[...] += a

    @pl.when(j == 2 * nchunk - 1)
    def _():
        o = acc_ref[...]
        pnew = pn_ref[...]
        for t in range(n_new):
            vtile = jnp.concatenate([vn_ref[t]] * n_new, axis=0)
            o = o + pnew[:, t:t + 1] * vtile
        y = _subln_gate(o, g_ref[...], sz_ref[...].reshape(nrow, V_DIM), lam_init)
        o_ref[...] = y.astype(o_ref.dtype).reshape(n_new, N_HEADS, V_DIM)


def _attn_decode(page_table, lam_vecs, qkv_s5, aux_s4, g_subln, ck, cv, page_base, lam_init, pps=4):
    n_new, dec_b = qkv_s5.shape[:2]
    n_pages = page_table.shape[1]
    nchunk = n_pages // pps
    nrow = n_new * N_HEADS
    ncol = cv.shape[1]
    sza0 = (2 * POOL_WIDTH) // (N_HEADS * V_DIM)

    def small(sel):
        return pl.BlockSpec((n_new, None, None, N_HEADS, V_DIM), lambda b, j, pt: (0, b, sel, 0, 0))

    def k_spec(i):
        return pl.BlockSpec(
            (None, 2 * ncol, HEAD_DIM),
            lambda b, j, pt: (page_base + pt[b, jnp.minimum(j, nchunk - 1) * pps + i], 0, 0))

    def v_spec(i):
        def index(b, j, pt):
            bb = jnp.where(j < nchunk, jnp.maximum(b - 1, 0), b)
            cc = jnp.where(j < nchunk, nchunk - 1, j - nchunk)
            return (page_base + pt[bb, cc * pps + i], 0, 0)
        return pl.BlockSpec((None, ncol, V_DIM), index)

    kern = functools.partial(_attn_decode_kernel, pps=pps, nchunk=nchunk, n_new=n_new,
                             lam_init=lam_init, scale=HEAD_DIM ** -0.5)
    grid_spec = pltpu.PrefetchScalarGridSpec(
        num_scalar_prefetch=1,
        grid=(dec_b, 2 * nchunk),
        in_specs=[pl.BlockSpec((4, HEAD_DIM), lambda b, j, pt: (0, 0)),
                  small(0), small(1), small(2),
                  pl.BlockSpec((n_new, None, N_HEADS, V_DIM), lambda b, j, pt: (0, b, sza0, 0)),
                  pl.BlockSpec((1, V_DIM), lambda b, j, pt: (0, 0))]
                 + [k_spec(i) for i in range(pps)] + [v_spec(i) for i in range(pps)],
        out_specs=pl.BlockSpec((n_new, None, N_HEADS, V_DIM), lambda b, j, pt: (0, b, 0, 0)),
        scratch_shapes=[pltpu.VMEM((nchunk, 2, nrow, pps * ncol), F32),
                        pltpu.VMEM((nchunk, nrow, pps * ncol), BF16),
                        pltpu.VMEM((nrow, LANES), F32),
                        pltpu.VMEM((nrow, V_DIM), F32)],
    )
    return pl.pallas_call(
        kern,
        grid_spec=grid_spec,
        out_shape=jax.ShapeDtypeStruct((n_new, dec_b, N_HEADS, V_DIM), F32),
        compiler_params=_params(("arbitrary", "arbitrary")),
        name="attn_decode",
    )(page_table, lam_vecs, qkv_s5, qkv_s5, qkv_s5, aux_s4, g_subln.reshape(1, V_DIM),
      *([ck] * pps), *([cv] * pps))


def _merge_kernel(po_ref, ao_ref, wpp_ref, wpa_ref, gp_ref, ga_ref, o_ref):
    pp = _dot(po_ref[...].astype(BF16), wpp_ref[...].astype(BF16))
    pa = _dot(ao_ref[...].astype(BF16), wpa_ref[...].astype(BF16))
    o_ref[...] = (gp_ref[...] * pp + ga_ref[...] * pa).astype(BF16)


def _merge(pool_out, attn_out, aux, w_proj_pool, w_proj_attn, tm, tn=512):
    m = pool_out.shape[0]
    gp0 = (2 * POOL_WIDTH + D_MODEL) // tn
    ga0 = gp0 + D_MODEL // tn
    return pl.pallas_call(
        _merge_kernel,
        grid=(m // tm, D_MODEL // tn),
        in_specs=[pl.BlockSpec((tm, POOL_WIDTH), lambda i, j: (i, 0)),
                  pl.BlockSpec((tm, D_MODEL), lambda i, j: (i, 0)),
                  pl.BlockSpec((POOL_WIDTH, tn), lambda i, j: (0, j)),
                  pl.BlockSpec((D_MODEL, tn), lambda i, j: (0, j)),
                  pl.BlockSpec((tm, tn), lambda i, j: (i, gp0 + j)),
                  pl.BlockSpec((tm, tn), lambda i, j: (i, ga0 + j))],
        out_specs=pl.BlockSpec((tm, tn), lambda i, j: (i, j)),
        out_shape=jax.ShapeDtypeStruct((m, D_MODEL), BF16),
        compiler_params=_params(("parallel", "arbitrary")),
        name="gated_merge",
    )(pool_out, attn_out, w_proj_pool, w_proj_attn, aux, aux)


def _out_kernel(m_ref, w_ref, x_ref, gate_ref, g_ref, y_ref, acc_ref, *, tn, nj):
    j = pl.program_id(1)
    acc_ref[j] = _dot(m_ref[...], w_ref[...].astype(BF16))

    @pl.when(j == nj - 1)
    def _():
        ss = None
        for c in range(nj):
            a = acc_ref[c]
            t = jnp.sum(a * a, axis=-1, keepdims=True)
            ss = t if ss is None else ss + t
        r = lax.rsqrt(ss / float(nj * tn) + EPS)
        for c in range(nj):
            sl = slice(c * tn, (c + 1) * tn)
            y_ref[:, sl] = x_ref[:, sl] + gate_ref[:, sl] * (acc_ref[c] * r * g_ref[:, sl])


def _out_proj(merged, w_out, x2d, mod, mod_spec2, g_post, tm, tn=512):
    m, d = x2d.shape
    nj = d // tn
    return pl.pallas_call(
        functools.partial(_out_kernel, tn=tn, nj=nj),
        grid=(m // tm, nj),
        in_specs=[pl.BlockSpec((tm, d), lambda i, j: (i, 0)),
                  pl.BlockSpec((d, tn), lambda i, j: (0, j)),
                  pl.BlockSpec((tm, d), lambda i, j: (i, 0)),
                  mod_spec2(2),
                  pl.BlockSpec((1, d), lambda i, j: (0, 0))],
        out_specs=pl.BlockSpec((tm, d), lambda i, j: (i, 0)),
        out_shape=jax.ShapeDtypeStruct((m, d), F32),
        scratch_shapes=[pltpu.VMEM((nj, tm, tn), F32)],
        compiler_params=_params(("parallel", "arbitrary")),
        name="out_proj",
    )(merged, w_out, x2d, mod, g_post.reshape(1, d))


def _rope_tables(pos):
    half = HEAD_DIM // 2
    inv_freq = ROPE_THETA ** (-jnp.arange(half, dtype=F32) / half)
    ang = pos.astype(F32)[:, None] * inv_freq[None, :]
    cos = jnp.cos(ang)
    sin = jnp.sin(ang)
    return jnp.concatenate([cos, cos], axis=1), jnp.concatenate([-sin, sin], axis=1)


def kernel(x_prompt, x_sample, cache_k, cache_v, state_pool, page_table, c_prompt, c_sample,
           w_ada, b_ada, g_pre, g_post, w_in, w_pool_grp, pool_scale,
           lambda_q1, lambda_k1, lambda_q2, lambda_k2, g_subln, w_proj_pool, w_proj_attn, w_out):
    batch, seq, d = x_prompt.shape
    dec_b, n_new, _ = x_sample.shape
    depth = w_in.shape[0]
    n_pages = page_table.shape[1]
    n_pool_pages, page_size = cache_k.shape[1], cache_k.shape[2]
    past_len = n_pages * page_size
    width = N_HEADS * V_DIM
    ck = cache_k.reshape(depth * n_pool_pages, page_size * N_HEADS * 2, HEAD_DIM)
    cv = cache_v.reshape(depth * n_pool_pages, page_size * N_HEADS, V_DIM)

    cos_p, sin_p = _rope_tables(jnp.arange(seq))
    cos_s, sin_s = _rope_tables(past_len + jnp.repeat(jnp.arange(n_new), dec_b))

    xp = x_prompt.reshape(batch * seq, d)
    xs = x_sample.transpose(1, 0, 2).reshape(n_new * dec_b, d)
    c_all = jnp.concatenate([c_prompt, c_sample], axis=0)
    tm_p = 1024
    tiles_per_seq = seq // tm_p

    outs = {name: [] for name in ("kp", "vp", "up", "ks", "vs", "us")}
    for l in range(depth):
        lam_init = 0.8 - 0.6 * math.exp(-0.3 * l)
        lam_vecs = jnp.stack([lambda_q1[l], lambda_k1[l], lambda_q2[l], lambda_k2[l]])
        mod = _mod(c_all, w_ada[l], b_ada[l])
        mod_p = mod[:batch].reshape(batch, 1, 3 * d)
        mod_s = mod[batch:]

        def spec_p(tm):
            per_seq = seq // tm
            return lambda col: pl.BlockSpec((None, 1, d), lambda i, *_: (i // per_seq, 0, col))

        def spec_s(col):
            return pl.BlockSpec((dec_b, d), lambda i, *_: (0, col))

        h = _modulated_norm(xp, g_pre[l], mod_p, spec_p(512), 512)
        aux = _aux_proj(h, w_in[l], tm_p)
        qkv16, k32, v32 = _qkv_proj(h, w_in[l], cos_p, sin_p, tm_p, BF16)
        pool_out = _pool_prompt(aux, w_pool_grp[l], pool_scale[l], seq)
        attn_out = _attn_prompt(lam_vecs, qkv16, aux, g_subln[l], batch, seq, lam_init)
        merged = _merge(pool_out, attn_out, aux, w_proj_pool[l], w_proj_attn[l], tm_p)
        xp_new = _out_proj(merged, w_out[l], xp, mod_p, spec_p(512), g_post[l], 512)
        outs["kp"].append(k32.reshape(batch, seq, N_HEADS, 2, HEAD_DIM))
        outs["vp"].append(v32.reshape(batch, seq, N_HEADS, V_DIM))
        outs["up"].append(aux.reshape(batch, seq, AUX_WIDTH)[:, seq - POOL_BUF:, :POOL_WIDTH])

        ms = n_new * dec_b
        h_s = _modulated_norm(xs, g_pre[l], mod_s, spec_s, dec_b)
        aux_s = _aux_proj(h_s, w_in[l], ms)
        qkv_s, k32_s, v32_s = _qkv_proj(h_s, w_in[l], cos_s, sin_s, ms, F32)
        sp_t = state_pool[l].transpose(1, 0, 2)
        pool_out_s = _pool_sample(sp_t, aux_s.reshape(n_new, dec_b, AUX_WIDTH), w_pool_grp[l], pool_scale[l])
        attn_out_s = _attn_decode(
            page_table, lam_vecs,
            qkv_s.reshape(n_new, dec_b, 3, N_HEADS, V_DIM),
            aux_s.reshape(n_new, dec_b, AUX_WIDTH // V_DIM, V_DIM),
            g_subln[l], ck, cv, l * n_pool_pages, lam_init)
        merged_s = _merge(pool_out_s, attn_out_s.reshape(ms, width), aux_s,
                          w_proj_pool[l], w_proj_attn[l], ms)
        xs_new = _out_proj(merged_s, w_out[l], xs, mod_s, spec_s, g_post[l], dec_b)

        def to_bt(a):
            return a.reshape(n_new, dec_b, -1).transpose(1, 0, 2)

        outs["ks"].append(to_bt(k32_s).reshape(dec_b, n_new, N_HEADS, 2, HEAD_DIM))
        outs["vs"].append(to_bt(v32_s).reshape(dec_b, n_new, N_HEADS, V_DIM))
        u_ext = jnp.concatenate([state_pool[l], to_bt(aux_s[:, :POOL_WIDTH])], axis=1)
        outs["us"].append(u_ext[:, -POOL_BUF:])
        xp, xs = xp_new, xs_new

    yp = xp.reshape(batch, seq, d)
    ys = xs.reshape(n_new, dec_b, d).transpose(1, 0, 2)
    return (yp, ys, jnp.stack(outs["kp"]), jnp.stack(outs["vp"]), jnp.stack(outs["up"]),
            jnp.stack(outs["ks"]), jnp.stack(outs["vs"]), jnp.stack(outs["us"]))
```

```python
import functools
import math

import jax
import jax.numpy as jnp
from jax import lax
from jax.experimental import pallas as pl
from jax.experimental.pallas import tpu as pltpu

F32 = jnp.float32
BF16 = jnp.bfloat16

EPS = 1e-6
ROPE_THETA = 10000.0
N_HEADS = 8
HEAD_DIM = 128
V_DIM = 2 * HEAD_DIM
D_MODEL = 2048
POOL_WIDTH = 1024
POOL_GROUP = 256
POOL_WINDOWS = (2, 4, 8, 16)
POOL_BUF = 15
LANES = 128
COL_CHUNK = 256
HALO = 16
AUX_WIDTH = 8192
VMEM_LIMIT = 56 * 1024 * 1024


def _params(sem):
    return pltpu.CompilerParams(dimension_semantics=sem, vmem_limit_bytes=VMEM_LIMIT)


def _dot(a, b):
    return jnp.dot(a, b, preferred_element_type=F32)


def _dot_nt(a, b):
    return lax.dot_general(a, b, (((1,), (1,)), ((), ())), preferred_element_type=F32)


def _sigmoid(x):
    return jax.nn.sigmoid(x)


def _mod_kernel(c_ref, w_ref, b_ref, o_ref):
    c = c_ref[...]
    a = (c * _sigmoid(c)).astype(BF16)
    o_ref[...] = _dot(a, w_ref[...].astype(BF16)) + b_ref[...]


def _mod(c_all, w_ada, b_ada, tn=512):
    m, d = c_all.shape
    n = w_ada.shape[1]
    return pl.pallas_call(
        _mod_kernel,
        grid=(n // tn,),
        in_specs=[pl.BlockSpec((m, d), lambda j: (0, 0)),
                  pl.BlockSpec((d, tn), lambda j: (0, j)),
                  pl.BlockSpec((1, tn), lambda j: (0, j))],
        out_specs=pl.BlockSpec((m, tn), lambda j: (0, j)),
        out_shape=jax.ShapeDtypeStruct((m, n), F32),
        compiler_params=_params(("parallel",)),
        name="adaln_mod",
    )(c_all, w_ada, b_ada.reshape(1, n))


def _h_kernel(x_ref, g_ref, sc_ref, sh_ref, o_ref):
    x = x_ref[...]
    ms = jnp.mean(x * x, axis=-1, keepdims=True)
    y = x * lax.rsqrt(ms + EPS) * g_ref[...]
    o_ref[...] = (y * (1.0 + sc_ref[...]) + sh_ref[...]).astype(BF16)


def _modulated_norm(x2d, g_pre, mod, mod_spec, tm):
    m, d = x2d.shape
    return pl.pallas_call(
        _h_kernel,
        grid=(m // tm,),
        in_specs=[pl.BlockSpec((tm, d), lambda i: (i, 0)),
                  pl.BlockSpec((1, d), lambda i: (0, 0)),
                  mod_spec(1), mod_spec(0)],
        out_specs=pl.BlockSpec((tm, d), lambda i: (i, 0)),
        out_shape=jax.ShapeDtypeStruct((m, d), BF16),
        compiler_params=_params(("parallel",)),
        name="modulated_norm",
    )(x2d, g_pre.reshape(1, d), mod, mod)


def _col_chunks(tn):
    return [(c * COL_CHUNK, COL_CHUNK) for c in range(tn // COL_CHUNK)]


def _aux_kernel(h_ref, w_ref, o_ref, *, tn):
    j = pl.program_id(1)
    n_plain = POOL_WIDTH // tn
    n_silu_end = (2 * POOL_WIDTH + D_MODEL) // tn

    def run(epilogue):
        for c0, cw in _col_chunks(tn):
            acc = _dot(h_ref[...], w_ref[:, c0:c0 + cw].astype(BF16))
            o_ref[:, c0:c0 + cw] = epilogue(acc)

    @pl.when(j < n_plain)
    def _():
        run(lambda a: a)

    @pl.when((j >= n_plain) & (j < n_silu_end))
    def _():
        run(lambda a: a * _sigmoid(a))

    @pl.when(j >= n_silu_end)
    def _():
        run(_sigmoid)


def _aux_proj(h, w_in, tm, tn=1024):
    m, d = h.shape
    n_head = (2 * POOL_WIDTH) // tn
    skip = (3 * D_MODEL) // tn
    return pl.pallas_call(
        functools.partial(_aux_kernel, tn=tn),
        grid=(m // tm, AUX_WIDTH // tn),
        in_specs=[pl.BlockSpec((tm, d), lambda i, j: (i, 0)),
                  pl.BlockSpec((d, tn), lambda i, j: (0, jnp.where(j < n_head, j, j + skip)))],
        out_specs=pl.BlockSpec((tm, tn), lambda i, j: (i, j)),
        out_shape=jax.ShapeDtypeStruct((m, AUX_WIDTH), F32),
        compiler_params=_params(("parallel", "arbitrary")),
        name="aux_proj",
    )(h, w_in)


def _rope(acc, cos, sin_signed):
    parts = []
    for c in range(acc.shape[1] // LANES):
        xs = acc[:, c * LANES:(c + 1) * LANES]
        parts.append(xs * cos + pltpu.roll(xs, HEAD_DIM // 2, axis=1) * sin_signed)
    return jnp.concatenate(parts, axis=1)


def _qkv_kernel(h_ref, w_ref, cos_ref, sin_ref, lo_ref, k_ref, v_ref, *, tm, tn):
    j = pl.program_id(1)
    nb = D_MODEL // tn
    n_hm = 2 * N_HEADS

    def chunk(c0, cw):
        return _dot(h_ref[...], w_ref[:, c0:c0 + cw].astype(BF16))

    @pl.when(j < nb)
    def _():
        for c0, cw in _col_chunks(tn):
            r = _rope(chunk(c0, cw), cos_ref[...], sin_ref[...])
            lo_ref[:, c0:c0 + cw] = r.astype(BF16).astype(lo_ref.dtype)

    @pl.when((j >= nb) & (j < 2 * nb))
    def _():
        for c0, cw in _col_chunks(tn):
            r = _rope(chunk(c0, cw), cos_ref[...], sin_ref[...])
            lo_ref[:, c0:c0 + cw] = r.astype(BF16).astype(lo_ref.dtype)
            for e in range(cw // LANES):
                hm = (j - nb) * (tn // LANES) + c0 // LANES + e
                k_ref[pl.ds(hm, tm, stride=n_hm), :] = r[:, e * LANES:(e + 1) * LANES]

    @pl.when(j >= 2 * nb)
    def _():
        for c0, cw in _col_chunks(tn):
            acc = chunk(c0, cw)
            v_ref[:, c0:c0 + cw] = acc
            lo_ref[:, c0:c0 + cw] = acc.astype(BF16).astype(lo_ref.dtype)


def _qkv_proj(h, w_in, cos2, sin2, tm, lo_dtype, tn=512):
    m, d = h.shape
    nb = D_MODEL // tn
    n_hm = 2 * N_HEADS
    col0 = (2 * POOL_WIDTH) // tn
    n_pos = cos2.shape[0] // tm
    return pl.pallas_call(
        functools.partial(_qkv_kernel, tm=tm, tn=tn),
        grid=(m // tm, 3 * nb),
        in_specs=[pl.BlockSpec((tm, d), lambda i, j: (i, 0)),
                  pl.BlockSpec((d, tn), lambda i, j: (0, col0 + j)),
                  pl.BlockSpec((tm, LANES), lambda i, j: (i % n_pos, 0)),
                  pl.BlockSpec((tm, LANES), lambda i, j: (i % n_pos, 0))],
        out_specs=[pl.BlockSpec((tm, tn), lambda i, j: (i, j)),
                   pl.BlockSpec((tm * n_hm, HEAD_DIM), lambda i, j: (i, 0)),
                   pl.BlockSpec((tm, tn), lambda i, j: (i, jnp.clip(j - 2 * nb, 0, nb - 1)))],
        out_shape=[jax.ShapeDtypeStruct((m, 3 * D_MODEL), lo_dtype),
                   jax.ShapeDtypeStruct((m * n_hm, HEAD_DIM), F32),
                   jax.ShapeDtypeStruct((m, D_MODEL), F32)],
        compiler_params=_params(("parallel", "arbitrary")),
        name="qkv_proj",
    )(h, w_in, cos2, sin2)


def _pool_mix(pooled_groups, wg_ref, scale_ref, szp):
    mixed = [_dot(p.astype(BF16), wg_ref[g].astype(BF16)) for g, p in enumerate(pooled_groups)]
    mixed = jnp.concatenate(mixed, axis=1)
    return (mixed * scale_ref[...] * szp).astype(BF16)


def _pool_prompt_kernel(u_ref, halo_ref, szp_ref, wg_ref, scale_ref, o_ref, *, tp, tiles_per_seq):
    i = pl.program_id(0)
    first = (i % tiles_per_seq) == 0
    halo = jnp.where(first, 0.0, halo_ref[...])
    ext = jnp.concatenate([halo, u_ref[...]], axis=0)
    pos = (i % tiles_per_seq) * tp + lax.broadcasted_iota(jnp.int32, (tp, 1), 0)
    groups = []
    for g, w in enumerate(POOL_WINDOWS):
        e = ext[:, g * POOL_GROUP:(g + 1) * POOL_GROUP]
        a = e
        s = 1
        while s < w:
            a = a + pltpu.roll(a, s, axis=0)
            s *= 2
        cnt = jnp.minimum(pos + 1, w).astype(F32)
        groups.append(a[HALO:] / cnt - e[HALO:])
    o_ref[...] = _pool_mix(groups, wg_ref, scale_ref, szp_ref[...])


def _pool_prompt(aux, w_pool_grp, pool_scale, seq, tp=512):
    m = aux.shape[0]
    tiles_per_seq = seq // tp
    hb = tp // HALO
    return pl.pallas_call(
        functools.partial(_pool_prompt_kernel, tp=tp, tiles_per_seq=tiles_per_seq),
        grid=(m // tp,),
        in_specs=[pl.BlockSpec((tp, POOL_WIDTH), lambda i: (i, 0)),
                  pl.BlockSpec((HALO, POOL_WIDTH), lambda i: (jnp.maximum(i * hb - 1, 0), 0)),
                  pl.BlockSpec((tp, POOL_WIDTH), lambda i: (i, 1)),
                  pl.BlockSpec((4, POOL_GROUP, POOL_GROUP), lambda i: (0, 0, 0)),
                  pl.BlockSpec((1, POOL_WIDTH), lambda i: (0, 0))],
        out_specs=pl.BlockSpec((tp, POOL_WIDTH), lambda i: (i, 0)),
        out_shape=jax.ShapeDtypeStruct((m, POOL_WIDTH), BF16),
        compiler_params=_params(("parallel",)),
        name="pool_prompt",
    )(aux, aux, aux, w_pool_grp, pool_scale.reshape(1, POOL_WIDTH))


def _pool_sample_kernel(sp_ref, u_ref, szp_ref, wg_ref, scale_ref, o_ref, *, n_new):
    nb = u_ref.shape[1]
    ext = [sp_ref[r] for r in range(POOL_BUF)] + [u_ref[t] for t in range(n_new)]
    groups = []
    for g, w in enumerate(POOL_WINDOWS):
        per_t = []
        for t in range(n_new):
            end = POOL_BUF + t
            acc = ext[end][:, g * POOL_GROUP:(g + 1) * POOL_GROUP]
            cur = acc
            for r in range(end - w + 1, end):
                acc = acc + ext[r][:, g * POOL_GROUP:(g + 1) * POOL_GROUP]
            per_t.append(acc / float(w) - cur)
        groups.append(jnp.concatenate(per_t, axis=0))
    szp = szp_ref[...].reshape(n_new * nb, POOL_WIDTH)
    o_ref[...] = _pool_mix(groups, wg_ref, scale_ref, szp)


def _pool_sample(sp_t, aux_s3, w_pool_grp, pool_scale):
    n_new, nb, _ = aux_s3.shape
    return pl.pallas_call(
        functools.partial(_pool_sample_kernel, n_new=n_new),
        grid=(1,),
        in_specs=[pl.BlockSpec((POOL_BUF, nb, POOL_WIDTH), lambda i: (0, 0, 0)),
                  pl.BlockSpec((n_new, nb, POOL_WIDTH), lambda i: (0, 0, 0)),
                  pl.BlockSpec((n_new, nb, POOL_WIDTH), lambda i: (0, 0, 1)),
                  pl.BlockSpec((4, POOL_GROUP, POOL_GROUP), lambda i: (0, 0, 0)),
                  pl.BlockSpec((1, POOL_WIDTH), lambda i: (0, 0))],
        out_specs=pl.BlockSpec((n_new * nb, POOL_WIDTH), lambda i: (0, 0)),
        out_shape=jax.ShapeDtypeStruct((n_new * nb, POOL_WIDTH), BF16),
        compiler_params=_params(("arbitrary",)),
        name="pool_sample",
    )(sp_t, aux_s3, aux_s3, w_pool_grp, pool_scale.reshape(1, POOL_WIDTH))


def _lambda(lv, lam_init):
    l1 = jnp.sum(lv[0:1] * lv[1:2], axis=1, keepdims=True)
    l2 = jnp.sum(lv[2:3] * lv[3:4], axis=1, keepdims=True)
    return jnp.exp(l1) - jnp.exp(l2) + lam_init


def _subln_gate(o, g, sz, lam_init):
    ms = jnp.mean(o * o, axis=-1, keepdims=True)
    y = o * lax.rsqrt(ms + EPS) * g
    return (y * (1.0 - lam_init) * sz).astype(BF16)


def _attn_prompt_kernel(lv_ref, q_ref, k_ref, v_ref, sz_ref, g_ref, o_ref, s_ref, p_ref,
                        *, tq, nq, lam_init, scale):
    qi = pl.program_id(2)
    lam = _lambda(lv_ref[...], lam_init)
    c_exp = scale * math.log2(math.e)

    def fold(x, op):
        r = x[:, :LANES]
        for c in range(1, tq // LANES):
            r = op(r, x[:, c * LANES:(c + 1) * LANES])
        return r

    def tile(n):
        row = lax.broadcasted_iota(jnp.int32, (tq, tq), 0)
        col = lax.broadcasted_iota(jnp.int32, (tq, tq), 1)
        pv, rinv = [], []
        for mp in range(2):
            q = q_ref[:, mp * HEAD_DIM:(mp + 1) * HEAD_DIM]
            m_run = None
            for kc in range(n + 1):
                s = _dot_nt(q, k_ref[kc * tq:(kc + 1) * tq, mp * HEAD_DIM:(mp + 1) * HEAD_DIM])
                if kc == n:
                    s = jnp.where(col <= row, s, -jnp.inf)
                s_ref[mp, :, kc * tq:(kc + 1) * tq] = s
                f = fold(s, jnp.maximum)
                m_run = f if m_run is None else jnp.maximum(m_run, f)
            mx = jnp.max(m_run, axis=1, keepdims=True)
            l_run = None
            for kc in range(n + 1):
                p = jnp.exp2((s_ref[mp, :, kc * tq:(kc + 1) * tq] - mx) * c_exp)
                p_ref[mp, :, kc * tq:(kc + 1) * tq] = p.astype(BF16)
                f = fold(p, jnp.add)
                l_run = f if l_run is None else l_run + f
            rinv.append(1.0 / jnp.sum(l_run, axis=1, keepdims=True))
            pv.append(_dot(p_ref[mp, :, 0:(n + 1) * tq], v_ref[0:(n + 1) * tq, :]))
        o = pv[0] * rinv[0] - pv[1] * (lam * rinv[1])
        o_ref[...] = _subln_gate(o, g_ref[...], sz_ref[...], lam_init)

    for n in range(nq):
        pl.when(qi == n)(functools.partial(tile, n))


def _attn_prompt(lam_vecs, qkv16, aux, g_subln, batch, seq, lam_init, tq=256):
    m = qkv16.shape[0]
    nq = seq // tq
    sza0 = (2 * POOL_WIDTH) // V_DIM
    kern = functools.partial(_attn_prompt_kernel, tq=tq, nq=nq, lam_init=lam_init, scale=HEAD_DIM ** -0.5)
    return pl.pallas_call(
        kern,
        grid=(batch, N_HEADS, nq),
        in_specs=[pl.BlockSpec((4, HEAD_DIM), lambda b, h, qi: (0, 0)),
                  pl.BlockSpec((tq, V_DIM), lambda b, h, qi: (b * nq + qi, h)),
                  pl.BlockSpec((seq, V_DIM), lambda b, h, qi: (b, N_HEADS + h)),
                  pl.BlockSpec((seq, V_DIM), lambda b, h, qi: (b, 2 * N_HEADS + h)),
                  pl.BlockSpec((tq, V_DIM), lambda b, h, qi: (b * nq + qi, sza0 + h)),
                  pl.BlockSpec((1, V_DIM), lambda b, h, qi: (0, 0))],
        out_specs=pl.BlockSpec((tq, V_DIM), lambda b, h, qi: (b * nq + qi, h)),
        out_shape=jax.ShapeDtypeStruct((m, N_HEADS * V_DIM), BF16),
        scratch_shapes=[pltpu.VMEM((2, tq, seq), F32),
                        pltpu.VMEM((2, tq, seq), BF16)],
        compiler_params=_params(("parallel", "parallel", "arbitrary")),
        name="attn_prompt",
    )(lam_vecs, qkv16, qkv16, qkv16, aux, g_subln.reshape(1, V_DIM))


def _attn_decode_kernel(pt_ref, lv_ref, q_ref, kn_ref, vn_ref, sz_ref, g_ref, *rest,
                        pps, nchunk, n_new, lam_init, scale):
    del pt_ref
    k_refs = rest[:pps]
    v_refs = rest[pps:2 * pps]
    o_ref = rest[2 * pps]
    s_ref, p_ref, pn_ref, acc_ref = rest[2 * pps + 1:]
    j = pl.program_id(1)
    nrow = n_new * N_HEADS
    ncol = k_refs[0].shape[0] // 2

    @pl.when(j < nchunk)
    def _():
        qq = q_ref[...].reshape(nrow, V_DIM)
        rows = lax.broadcasted_iota(jnp.int32, (nrow, ncol), 0)
        cols = lax.broadcasted_iota(jnp.int32, (nrow, ncol), 1)
        valid = (cols & (N_HEADS - 1)) == (rows & (N_HEADS - 1))
        for mp in range(2):
            qm = qq[:, mp * HEAD_DIM:(mp + 1) * HEAD_DIM].astype(BF16)
            for i in range(pps):
                km = k_refs[i][pl.ds(mp, ncol, stride=2), :].astype(BF16)
                s = _dot_nt(qm, km) * scale
                s_ref[j, mp, :, i * ncol:(i + 1) * ncol] = jnp.where(valid, s, -jnp.inf)

    @pl.when(j == nchunk - 1)
    def _():
        lam = _lambda(lv_ref[...], lam_init)
        qq = q_ref[...].reshape(nrow, V_DIM)
        lane = lax.broadcasted_iota(jnp.int32, (nrow, LANES), 1)
        qrow = lax.broadcasted_iota(jnp.int32, (nrow, LANES), 0) >> 3
        sn = [jnp.full((nrow, LANES), -jnp.inf, F32) for _ in range(2)]
        for t in range(n_new):
            ktile = jnp.concatenate([kn_ref[t]] * n_new, axis=0)
            prod = qq * ktile
            for mp in range(2):
                col = jnp.sum(prod[:, mp * HEAD_DIM:(mp + 1) * HEAD_DIM], axis=1, keepdims=True) * scale
                sn[mp] = jnp.where((lane == t) & (qrow >= t), col, sn[mp])
        pn = []
        rinv = []
        for mp in range(2):
            mx = jnp.max(sn[mp], axis=1, keepdims=True)
            for c in range(nchunk):
                mx = jnp.maximum(mx, jnp.max(s_ref[c, mp], axis=1, keepdims=True))
            pnew = jnp.exp(sn[mp] - mx)
            l = jnp.sum(pnew, axis=1, keepdims=True)
            for c in range(nchunk):
                p = jnp.exp(s_ref[c, mp] - mx)
                s_ref[c, mp] = p
                l = l + jnp.sum(p, axis=1, keepdims=True)
            pn.append(pnew)
            rinv.append(1.0 / l)
        r1 = rinv[0]
        r2 = lam * rinv[1]
        for c in range(nchunk):
            p_ref[c] = (s_ref[c, 0] * r1 - s_ref[c, 1] * r2).astype(BF16)
        pn_ref[...] = (pn[0] * r1 - pn[1] * r2).astype(BF16).astype(F32)

    @pl.when(j == nchunk)
    def _():
        acc_ref[...] = jnp.zeros_like(acc_ref)

    @pl.when(j >= nchunk)
    def _():
        c = j - nchunk
        a = None
        for i in range(pps):
            vp = v_refs[i][...].astype(BF16)
            d = _dot(p_ref[c, :, i * ncol:(i + 1) * ncol], vp)
            a = d if a is None else a + d
        acc_ref[...] += a

    @pl.when(j == 2 * nchunk - 1)
    def _():
        o = acc_ref[...]
        pnew = pn_ref[...]
        for t in range(n_new):
            vtile = jnp.concatenate([vn_ref[t]] * n_new, axis=0)
            o = o + pnew[:, t:t + 1] * vtile
        y = _subln_gate(o, g_ref[...], sz_ref[...].reshape(nrow, V_DIM), lam_init)
        o_ref[...] = y.astype(o_ref.dtype).reshape(n_new, N_HEADS, V_DIM)


def _attn_decode(page_table, lam_vecs, qkv_s5, aux_s4, g_subln, ck, cv, page_base, lam_init, pps=8):
    n_new, dec_b = qkv_s5.shape[:2]
    n_pages = page_table.shape[1]
    nchunk = n_pages // pps
    nrow = n_new * N_HEADS
    ncol = cv.shape[1]
    sza0 = (2 * POOL_WIDTH) // (N_HEADS * V_DIM)

    def small(sel):
        return pl.BlockSpec((n_new, None, None, N_HEADS, V_DIM), lambda b, j, pt: (0, b, sel, 0, 0))

    def k_spec(i):
        return pl.BlockSpec(
            (None, 2 * ncol, HEAD_DIM),
            lambda b, j, pt: (page_base + pt[b, jnp.minimum(j, nchunk - 1) * pps + i], 0, 0))

    def v_spec(i):
        def index(b, j, pt):
            bb = jnp.where(j < nchunk, jnp.maximum(b - 1, 0), b)
            cc = jnp.where(j < nchunk, nchunk - 1, j - nchunk)
            return (page_base + pt[bb, cc * pps + i], 0, 0)
        return pl.BlockSpec((None, ncol, V_DIM), index)

    kern = functools.partial(_attn_decode_kernel, pps=pps, nchunk=nchunk, n_new=n_new,
                             lam_init=lam_init, scale=HEAD_DIM ** -0.5)
    grid_spec = pltpu.PrefetchScalarGridSpec(
        num_scalar_prefetch=1,
        grid=(dec_b, 2 * nchunk),
        in_specs=[pl.BlockSpec((4, HEAD_DIM), lambda b, j, pt: (0, 0)),
                  small(0), small(1), small(2),
                  pl.BlockSpec((n_new, None, N_HEADS, V_DIM), lambda b, j, pt: (0, b, sza0, 0)),
                  pl.BlockSpec((1, V_DIM), lambda b, j, pt: (0, 0))]
                 + [k_spec(i) for i in range(pps)] + [v_spec(i) for i in range(pps)],
        out_specs=pl.BlockSpec((n_new, None, N_HEADS, V_DIM), lambda b, j, pt: (0, b, 0, 0)),
        scratch_shapes=[pltpu.VMEM((nchunk, 2, nrow, pps * ncol), F32),
                        pltpu.VMEM((nchunk, nrow, pps * ncol), BF16),
                        pltpu.VMEM((nrow, LANES), F32),
                        pltpu.VMEM((nrow, V_DIM), F32)],
    )
    return pl.pallas_call(
        kern,
        grid_spec=grid_spec,
        out_shape=jax.ShapeDtypeStruct((n_new, dec_b, N_HEADS, V_DIM), F32),
        compiler_params=_params(("arbitrary", "arbitrary")),
        name="attn_decode",
    )(page_table, lam_vecs, qkv_s5, qkv_s5, qkv_s5, aux_s4, g_subln.reshape(1, V_DIM),
      *([ck] * pps), *([cv] * pps))


def _merge_kernel(po_ref, ao_ref, wpp_ref, wpa_ref, gp_ref, ga_ref, o_ref):
    pp = _dot(po_ref[...].astype(BF16), wpp_ref[...].astype(BF16))
    pa = _dot(ao_ref[...].astype(BF16), wpa_ref[...].astype(BF16))
    o_ref[...] = (gp_ref[...] * pp + ga_ref[...] * pa).astype(BF16)


def _merge(pool_out, attn_out, aux, w_proj_pool, w_proj_attn, tm, tn=512):
    m = pool_out.shape[0]
    gp0 = (2 * POOL_WIDTH + D_MODEL) // tn
    ga0 = gp0 + D_MODEL // tn
    return pl.pallas_call(
        _merge_kernel,
        grid=(m // tm, D_MODEL // tn),
        in_specs=[pl.BlockSpec((tm, POOL_WIDTH), lambda i, j: (i, 0)),
                  pl.BlockSpec((tm, D_MODEL), lambda i, j: (i, 0)),
                  pl.BlockSpec((POOL_WIDTH, tn), lambda i, j: (0, j)),
                  pl.BlockSpec((D_MODEL, tn), lambda i, j: (0, j)),
                  pl.BlockSpec((tm, tn), lambda i, j: (i, gp0 + j)),
                  pl.BlockSpec((tm, tn), lambda i, j: (i, ga0 + j))],
        out_specs=pl.BlockSpec((tm, tn), lambda i, j: (i, j)),
        out_shape=jax.ShapeDtypeStruct((m, D_MODEL), BF16),
        compiler_params=_params(("parallel", "arbitrary")),
        name="gated_merge",
    )(pool_out, attn_out, w_proj_pool, w_proj_attn, aux, aux)


def _out_kernel(m_ref, w_ref, x_ref, gate_ref, g_ref, y_ref, acc_ref, *, tn, nj):
    j = pl.program_id(1)
    acc_ref[j] = _dot(m_ref[...], w_ref[...].astype(BF16))

    @pl.when(j == nj - 1)
    def _():
        ss = None
        for c in range(nj):
            a = acc_ref[c]
            t = jnp.sum(a * a, axis=-1, keepdims=True)
            ss = t if ss is None else ss + t
        r = lax.rsqrt(ss / float(nj * tn) + EPS)
        for c in range(nj):
            sl = slice(c * tn, (c + 1) * tn)
            y_ref[:, sl] = x_ref[:, sl] + gate_ref[:, sl] * (acc_ref[c] * r * g_ref[:, sl])


def _out_proj(merged, w_out, x2d, mod, mod_spec2, g_post, tm, tn=512):
    m, d = x2d.shape
    nj = d // tn
    return pl.pallas_call(
        functools.partial(_out_kernel, tn=tn, nj=nj),
        grid=(m // tm, nj),
        in_specs=[pl.BlockSpec((tm, d), lambda i, j: (i, 0)),
                  pl.BlockSpec((d, tn), lambda i, j: (0, j)),
                  pl.BlockSpec((tm, d), lambda i, j: (i, 0)),
                  mod_spec2(2),
                  pl.BlockSpec((1, d), lambda i, j: (0, 0))],
        out_specs=pl.BlockSpec((tm, d), lambda i, j: (i, 0)),
        out_shape=jax.ShapeDtypeStruct((m, d), F32),
        scratch_shapes=[pltpu.VMEM((nj, tm, tn), F32)],
        compiler_params=_params(("parallel", "arbitrary")),
        name="out_proj",
    )(merged, w_out, x2d, mod, g_post.reshape(1, d))


def _rope_tables(pos):
    half = HEAD_DIM // 2
    inv_freq = ROPE_THETA ** (-jnp.arange(half, dtype=F32) / half)
    ang = pos.astype(F32)[:, None] * inv_freq[None, :]
    cos = jnp.cos(ang)
    sin = jnp.sin(ang)
    return jnp.concatenate([cos, cos], axis=1), jnp.concatenate([-sin, sin], axis=1)


def kernel(x_prompt, x_sample, cache_k, cache_v, state_pool, page_table, c_prompt, c_sample,
           w_ada, b_ada, g_pre, g_post, w_in, w_pool_grp, pool_scale,
           lambda_q1, lambda_k1, lambda_q2, lambda_k2, g_subln, w_proj_pool, w_proj_attn, w_out):
    batch, seq, d = x_prompt.shape
    dec_b, n_new, _ = x_sample.shape
    depth = w_in.shape[0]
    n_pages = page_table.shape[1]
    n_pool_pages, page_size = cache_k.shape[1], cache_k.shape[2]
    past_len = n_pages * page_size
    width = N_HEADS * V_DIM
    ck = cache_k.reshape(depth * n_pool_pages, page_size * N_HEADS * 2, HEAD_DIM)
    cv = cache_v.reshape(depth * n_pool_pages, page_size * N_HEADS, V_DIM)

    cos_p, sin_p = _rope_tables(jnp.arange(seq))
    cos_s, sin_s = _rope_tables(past_len + jnp.repeat(jnp.arange(n_new), dec_b))

    xp = x_prompt.reshape(batch * seq, d)
    xs = x_sample.transpose(1, 0, 2).reshape(n_new * dec_b, d)
    c_all = jnp.concatenate([c_prompt, c_sample], axis=0)
    tm_p = 1024
    tiles_per_seq = seq // tm_p

    outs = {name: [] for name in ("kp", "vp", "up", "ks", "vs", "us")}
    for l in range(depth):
        lam_init = 0.8 - 0.6 * math.exp(-0.3 * l)
        lam_vecs = jnp.stack([lambda_q1[l], lambda_k1[l], lambda_q2[l], lambda_k2[l]])
        mod = _mod(c_all, w_ada[l], b_ada[l])
        mod_p = mod[:batch].reshape(batch, 1, 3 * d)
        mod_s = mod[batch:]

        def spec_p(tm):
            per_seq = seq // tm
            return lambda col: pl.BlockSpec((None, 1, d), lambda i, *_: (i // per_seq, 0, col))

        def spec_s(col):
            return pl.BlockSpec((dec_b, d), lambda i, *_: (0, col))

        h = _modulated_norm(xp, g_pre[l], mod_p, spec_p(512), 512)
        aux = _aux_proj(h, w_in[l], tm_p)
        qkv16, k32, v32 = _qkv_proj(h, w_in[l], cos_p, sin_p, tm_p, BF16)
        pool_out = _pool_prompt(aux, w_pool_grp[l], pool_scale[l], seq)
        attn_out = _attn_prompt(lam_vecs, qkv16, aux, g_subln[l], batch, seq, lam_init)
        merged = _merge(pool_out, attn_out, aux, w_proj_pool[l], w_proj_attn[l], tm_p)
        xp_new = _out_proj(merged, w_out[l], xp, mod_p, spec_p(512), g_post[l], 512)
        outs["kp"].append(k32.reshape(batch, seq, N_HEADS, 2, HEAD_DIM))
        outs["vp"].append(v32.reshape(batch, seq, N_HEADS, V_DIM))
        outs["up"].append(aux.reshape(batch, seq, AUX_WIDTH)[:, seq - POOL_BUF:, :POOL_WIDTH])

        ms = n_new * dec_b
        h_s = _modulated_norm(xs, g_pre[l], mod_s, spec_s, dec_b)
        aux_s = _aux_proj(h_s, w_in[l], ms)
        qkv_s, k32_s, v32_s = _qkv_proj(h_s, w_in[l], cos_s, sin_s, ms, F32)
        sp_t = state_pool[l].transpose(1, 0, 2)
        pool_out_s = _pool_sample(sp_t, aux_s.reshape(n_new, dec_b, AUX_WIDTH), w_pool_grp[l], pool_scale[l])
        attn_out_s = _attn_decode(
            page_table, lam_vecs,
            qkv_s.reshape(n_new, dec_b, 3, N_HEADS, V_DIM),
            aux_s.reshape(n_new, dec_b, AUX_WIDTH // V_DIM, V_DIM),
            g_subln[l], ck, cv, l * n_pool_pages, lam_init)
        merged_s = _merge(pool_out_s, attn_out_s.reshape(ms, width), aux_s,
                          w_proj_pool[l], w_proj_attn[l], ms)
        xs_new = _out_proj(merged_s, w_out[l], xs, mod_s, spec_s, g_post[l], dec_b)

        def to_bt(a):
            return a.reshape(n_new, dec_b, -1).transpose(1, 0, 2)

        outs["ks"].append(to_bt(k32_s).reshape(dec_b, n_new, N_HEADS, 2, HEAD_DIM))
        outs["vs"].append(to_bt(v32_s).reshape(dec_b, n_new, N_HEADS, V_DIM))
        u_ext = jnp.concatenate([state_pool[l], to_bt(aux_s[:, :POOL_WIDTH])], axis=1)
        outs["us"].append(u_ext[:, -POOL_BUF:])
        xp, xs = xp_new, xs_new

    yp = xp.reshape(batch, seq, d)
    ys = xs.reshape(n_new, dec_b, d).transpose(1, 0, 2)
    return (yp, ys, jnp.stack(outs["kp"]), jnp.stack(outs["vp"]), jnp.stack(outs["up"]),
            jnp.stack(outs["ks"]), jnp.stack(outs["vs"]), jnp.stack(outs["us"]))
```

```python
import functools
import math

import jax
import jax.numpy as jnp
from jax import lax
from jax.experimental import pallas as pl
from jax.experimental.pallas import tpu as pltpu

F32 = jnp.float32
BF16 = jnp.bfloat16

EPS = 1e-6
ROPE_THETA = 10000.0
N_HEADS = 8
HEAD_DIM = 128
V_DIM = 2 * HEAD_DIM
D_MODEL = 2048
POOL_WIDTH = 1024
POOL_GROUP = 256
POOL_WINDOWS = (2, 4, 8, 16)
POOL_BUF = 15
LANES = 128
COL_CHUNK = 256
PAGE_LOOKAHEAD = 3
PAGE_SLOTS = PAGE_LOOKAHEAD + 1
HALO = 16
AUX_WIDTH = 8192
VMEM_LIMIT = 56 * 1024 * 1024


def _params(sem):
    return pltpu.CompilerParams(dimension_semantics=sem, vmem_limit_bytes=VMEM_LIMIT)


def _dot(a, b):
    return jnp.dot(a, b, preferred_element_type=F32)


def _dot_nt(a, b):
    return lax.dot_general(a, b, (((1,), (1,)), ((), ())), preferred_element_type=F32)


def _sigmoid(x):
    return jax.nn.sigmoid(x)


def _mod_kernel(c_ref, w_ref, b_ref, o_ref):
    c = c_ref[...]
    a = (c * _sigmoid(c)).astype(BF16)
    o_ref[...] = _dot(a, w_ref[...].astype(BF16)) + b_ref[...]


def _mod(c_all, w_ada, b_ada, tn=512):
    m, d = c_all.shape
    n = w_ada.shape[1]
    return pl.pallas_call(
        _mod_kernel,
        grid=(n // tn,),
        in_specs=[pl.BlockSpec((m, d), lambda j: (0, 0)),
                  pl.BlockSpec((d, tn), lambda j: (0, j)),
                  pl.BlockSpec((1, tn), lambda j: (0, j))],
        out_specs=pl.BlockSpec((m, tn), lambda j: (0, j)),
        out_shape=jax.ShapeDtypeStruct((m, n), F32),
        compiler_params=_params(("parallel",)),
        name="adaln_mod",
    )(c_all, w_ada, b_ada.reshape(1, n))


def _h_kernel(x_ref, g_ref, sc_ref, sh_ref, o_ref):
    x = x_ref[...]
    ms = jnp.mean(x * x, axis=-1, keepdims=True)
    y = x * lax.rsqrt(ms + EPS) * g_ref[...]
    o_ref[...] = (y * (1.0 + sc_ref[...]) + sh_ref[...]).astype(BF16)


def _modulated_norm(x2d, g_pre, mod, mod_spec, tm):
    m, d = x2d.shape
    return pl.pallas_call(
        _h_kernel,
        grid=(m // tm,),
        in_specs=[pl.BlockSpec((tm, d), lambda i: (i, 0)),
                  pl.BlockSpec((1, d), lambda i: (0, 0)),
                  mod_spec(1), mod_spec(0)],
        out_specs=pl.BlockSpec((tm, d), lambda i: (i, 0)),
        out_shape=jax.ShapeDtypeStruct((m, d), BF16),
        compiler_params=_params(("parallel",)),
        name="modulated_norm",
    )(x2d, g_pre.reshape(1, d), mod, mod)


def _col_chunks(tn):
    return [(c * COL_CHUNK, COL_CHUNK) for c in range(tn // COL_CHUNK)]


def _aux_kernel(h_ref, w_ref, o_ref, *, tn):
    j = pl.program_id(1)
    n_plain = POOL_WIDTH // tn
    n_silu_end = (2 * POOL_WIDTH + D_MODEL) // tn

    def run(epilogue):
        for c0, cw in _col_chunks(tn):
            acc = _dot(h_ref[...], w_ref[:, c0:c0 + cw].astype(BF16))
            o_ref[:, c0:c0 + cw] = epilogue(acc)

    @pl.when(j < n_plain)
    def _():
        run(lambda a: a)

    @pl.when((j >= n_plain) & (j < n_silu_end))
    def _():
        run(lambda a: a * _sigmoid(a))

    @pl.when(j >= n_silu_end)
    def _():
        run(_sigmoid)


def _aux_proj(h, w_in, tm, tn=1024):
    m, d = h.shape
    n_head = (2 * POOL_WIDTH) // tn
    skip = (3 * D_MODEL) // tn
    return pl.pallas_call(
        functools.partial(_aux_kernel, tn=tn),
        grid=(m // tm, AUX_WIDTH // tn),
        in_specs=[pl.BlockSpec((tm, d), lambda i, j: (i, 0)),
                  pl.BlockSpec((d, tn), lambda i, j: (0, jnp.where(j < n_head, j, j + skip)))],
        out_specs=pl.BlockSpec((tm, tn), lambda i, j: (i, j)),
        out_shape=jax.ShapeDtypeStruct((m, AUX_WIDTH), F32),
        compiler_params=_params(("parallel", "arbitrary")),
        name="aux_proj",
    )(h, w_in)


def _rope(acc, cos, sin_signed):
    parts = []
    for c in range(acc.shape[1] // LANES):
        xs = acc[:, c * LANES:(c + 1) * LANES]
        parts.append(xs * cos + pltpu.roll(xs, HEAD_DIM // 2, axis=1) * sin_signed)
    return jnp.concatenate(parts, axis=1)


def _qkv_kernel(h_ref, w_ref, cos_ref, sin_ref, lo_ref, k_ref, v_ref, *, tm, tn):
    j = pl.program_id(1)
    nb = D_MODEL // tn
    n_hm = 2 * N_HEADS

    def chunk(c0, cw):
        return _dot(h_ref[...], w_ref[:, c0:c0 + cw].astype(BF16))

    @pl.when(j < nb)
    def _():
        for c0, cw in _col_chunks(tn):
            r = _rope(chunk(c0, cw), cos_ref[...], sin_ref[...])
            lo_ref[:, c0:c0 + cw] = r.astype(BF16).astype(lo_ref.dtype)

    @pl.when((j >= nb) & (j < 2 * nb))
    def _():
        for c0, cw in _col_chunks(tn):
            r = _rope(chunk(c0, cw), cos_ref[...], sin_ref[...])
            lo_ref[:, c0:c0 + cw] = r.astype(BF16).astype(lo_ref.dtype)
            for e in range(cw // LANES):
                hm = (j - nb) * (tn // LANES) + c0 // LANES + e
                k_ref[pl.ds(hm, tm, stride=n_hm), :] = r[:, e * LANES:(e + 1) * LANES]

    @pl.when(j >= 2 * nb)
    def _():
        for c0, cw in _col_chunks(tn):
            acc = chunk(c0, cw)
            v_ref[:, c0:c0 + cw] = acc
            lo_ref[:, c0:c0 + cw] = acc.astype(BF16).astype(lo_ref.dtype)


def _qkv_proj(h, w_in, cos2, sin2, tm, lo_dtype, tn=512):
    m, d = h.shape
    nb = D_MODEL // tn
    n_hm = 2 * N_HEADS
    col0 = (2 * POOL_WIDTH) // tn
    n_pos = cos2.shape[0] // tm
    return pl.pallas_call(
        functools.partial(_qkv_kernel, tm=tm, tn=tn),
        grid=(m // tm, 3 * nb),
        in_specs=[pl.BlockSpec((tm, d), lambda i, j: (i, 0)),
                  pl.BlockSpec((d, tn), lambda i, j: (0, col0 + j)),
                  pl.BlockSpec((tm, LANES), lambda i, j: (i % n_pos, 0)),
                  pl.BlockSpec((tm, LANES), lambda i, j: (i % n_pos, 0))],
        out_specs=[pl.BlockSpec((tm, tn), lambda i, j: (i, j)),
                   pl.BlockSpec((tm * n_hm, HEAD_DIM), lambda i, j: (i, 0)),
                   pl.BlockSpec((tm, tn), lambda i, j: (i, jnp.clip(j - 2 * nb, 0, nb - 1)))],
        out_shape=[jax.ShapeDtypeStruct((m, 3 * D_MODEL), lo_dtype),
                   jax.ShapeDtypeStruct((m * n_hm, HEAD_DIM), F32),
                   jax.ShapeDtypeStruct((m, D_MODEL), F32)],
        compiler_params=_params(("parallel", "arbitrary")),
        name="qkv_proj",
    )(h, w_in, cos2, sin2)


def _pool_mix(pooled_groups, wg_ref, scale_ref, szp):
    mixed = [_dot(p.astype(BF16), wg_ref[g].astype(BF16)) for g, p in enumerate(pooled_groups)]
    mixed = jnp.concatenate(mixed, axis=1)
    return (mixed * scale_ref[...] * szp).astype(BF16)


def _pool_prompt_kernel(u_ref, halo_ref, szp_ref, wg_ref, scale_ref, o_ref, *, tp, tiles_per_seq):
    i = pl.program_id(0)
    first = (i % tiles_per_seq) == 0
    halo = jnp.where(first, 0.0, halo_ref[...])
    ext = jnp.concatenate([halo, u_ref[...]], axis=0)
    pos = (i % tiles_per_seq) * tp + lax.broadcasted_iota(jnp.int32, (tp, 1), 0)
    groups = []
    for g, w in enumerate(POOL_WINDOWS):
        e = ext[:, g * POOL_GROUP:(g + 1) * POOL_GROUP]
        a = e
        s = 1
        while s < w:
            a = a + pltpu.roll(a, s, axis=0)
            s *= 2
        cnt = jnp.minimum(pos + 1, w).astype(F32)
        groups.append(a[HALO:] / cnt - e[HALO:])
    o_ref[...] = _pool_mix(groups, wg_ref, scale_ref, szp_ref[...])


def _pool_prompt(aux, w_pool_grp, pool_scale, seq, tp=512):
    m = aux.shape[0]
    tiles_per_seq = seq // tp
    hb = tp // HALO
    return pl.pallas_call(
        functools.partial(_pool_prompt_kernel, tp=tp, tiles_per_seq=tiles_per_seq),
        grid=(m // tp,),
        in_specs=[pl.BlockSpec((tp, POOL_WIDTH), lambda i: (i, 0)),
                  pl.BlockSpec((HALO, POOL_WIDTH), lambda i: (jnp.maximum(i * hb - 1, 0), 0)),
                  pl.BlockSpec((tp, POOL_WIDTH), lambda i: (i, 1)),
                  pl.BlockSpec((4, POOL_GROUP, POOL_GROUP), lambda i: (0, 0, 0)),
                  pl.BlockSpec((1, POOL_WIDTH), lambda i: (0, 0))],
        out_specs=pl.BlockSpec((tp, POOL_WIDTH), lambda i: (i, 0)),
        out_shape=jax.ShapeDtypeStruct((m, POOL_WIDTH), BF16),
        compiler_params=_params(("parallel",)),
        name="pool_prompt",
    )(aux, aux, aux, w_pool_grp, pool_scale.reshape(1, POOL_WIDTH))


def _pool_sample_kernel(sp_ref, u_ref, szp_ref, wg_ref, scale_ref, o_ref, *, n_new):
    nb = u_ref.shape[1]
    ext = [sp_ref[r] for r in range(POOL_BUF)] + [u_ref[t] for t in range(n_new)]
    groups = []
    for g, w in enumerate(POOL_WINDOWS):
        per_t = []
        for t in range(n_new):
            end = POOL_BUF + t
            acc = ext[end][:, g * POOL_GROUP:(g + 1) * POOL_GROUP]
            cur = acc
            for r in range(end - w + 1, end):
                acc = acc + ext[r][:, g * POOL_GROUP:(g + 1) * POOL_GROUP]
            per_t.append(acc / float(w) - cur)
        groups.append(jnp.concatenate(per_t, axis=0))
    szp = szp_ref[...].reshape(n_new * nb, POOL_WIDTH)
    o_ref[...] = _pool_mix(groups, wg_ref, scale_ref, szp)


def _pool_sample(sp_t, aux_s3, w_pool_grp, pool_scale):
    n_new, nb, _ = aux_s3.shape
    return pl.pallas_call(
        functools.partial(_pool_sample_kernel, n_new=n_new),
        grid=(1,),
        in_specs=[pl.BlockSpec((POOL_BUF, nb, POOL_WIDTH), lambda i: (0, 0, 0)),
                  pl.BlockSpec((n_new, nb, POOL_WIDTH), lambda i: (0, 0, 0)),
                  pl.BlockSpec((n_new, nb, POOL_WIDTH), lambda i: (0, 0, 1)),
                  pl.BlockSpec((4, POOL_GROUP, POOL_GROUP), lambda i: (0, 0, 0)),
                  pl.BlockSpec((1, POOL_WIDTH), lambda i: (0, 0))],
        out_specs=pl.BlockSpec((n_new * nb, POOL_WIDTH), lambda i: (0, 0)),
        out_shape=jax.ShapeDtypeStruct((n_new * nb, POOL_WIDTH), BF16),
        compiler_params=_params(("arbitrary",)),
        name="pool_sample",
    )(sp_t, aux_s3, aux_s3, w_pool_grp, pool_scale.reshape(1, POOL_WIDTH))


def _lambda(lv, lam_init):
    l1 = jnp.sum(lv[0:1] * lv[1:2], axis=1, keepdims=True)
    l2 = jnp.sum(lv[2:3] * lv[3:4], axis=1, keepdims=True)
    return jnp.exp(l1) - jnp.exp(l2) + lam_init


def _subln_gate(o, g, sz, lam_init):
    ms = jnp.mean(o * o, axis=-1, keepdims=True)
    y = o * lax.rsqrt(ms + EPS) * g
    return (y * (1.0 - lam_init) * sz).astype(BF16)


def _attn_prompt_kernel(lv_ref, q_ref, k_ref, v_ref, sz_ref, g_ref, o_ref, s_ref,
                        *, tq, nq, lam_init, scale):
    qi = pl.program_id(2)
    lam = _lambda(lv_ref[...], lam_init)
    c_exp = scale * math.log2(math.e)

    def fold(x, op):
        r = x[:, :LANES]
        for c in range(1, tq // LANES):
            r = op(r, x[:, c * LANES:(c + 1) * LANES])
        return r

    def tile(n):
        row = lax.broadcasted_iota(jnp.int32, (tq, tq), 0)
        col = lax.broadcasted_iota(jnp.int32, (tq, tq), 1)
        pv, rinv = [], []
        q = [q_ref[:, mp * HEAD_DIM:(mp + 1) * HEAD_DIM] for mp in range(2)]
        m_run = [None, None]
        for kc in range(n + 1):
            for mp in range(2):
                s = _dot_nt(q[mp], k_ref[kc * tq:(kc + 1) * tq, mp * HEAD_DIM:(mp + 1) * HEAD_DIM])
                if kc == n:
                    s = jnp.where(col <= row, s, -jnp.inf)
                s_ref[mp, :, kc * tq:(kc + 1) * tq] = s
                f = fold(s, jnp.maximum)
                m_run[mp] = f if m_run[mp] is None else jnp.maximum(m_run[mp], f)
        for mp in range(2):
            mx = jnp.max(m_run[mp], axis=1, keepdims=True)
            l_run = None
            acc = None
            for kc in range(n + 1):
                p = jnp.exp2((s_ref[mp, :, kc * tq:(kc + 1) * tq] - mx) * c_exp)
                f = fold(p, jnp.add)
                l_run = f if l_run is None else l_run + f
                d = _dot(p.astype(BF16), v_ref[kc * tq:(kc + 1) * tq, :])
                acc = d if acc is None else acc + d
            rinv.append(1.0 / jnp.sum(l_run, axis=1, keepdims=True))
            pv.append(acc)
        o = pv[0] * rinv[0] - pv[1] * (lam * rinv[1])
        o_ref[...] = _subln_gate(o, g_ref[...], sz_ref[...], lam_init)

    for n in range(nq):
        pl.when(qi == n)(functools.partial(tile, n))


def _attn_prompt(lam_vecs, qkv16, aux, g_subln, batch, seq, lam_init, tq=256):
    m = qkv16.shape[0]
    nq = seq // tq
    sza0 = (2 * POOL_WIDTH) // V_DIM
    kern = functools.partial(_attn_prompt_kernel, tq=tq, nq=nq, lam_init=lam_init, scale=HEAD_DIM ** -0.5)
    return pl.pallas_call(
        kern,
        grid=(batch, N_HEADS, nq),
        in_specs=[pl.BlockSpec((4, HEAD_DIM), lambda b, h, qi: (0, 0)),
                  pl.BlockSpec((tq, V_DIM), lambda b, h, qi: (b * nq + qi, h)),
                  pl.BlockSpec((seq, V_DIM), lambda b, h, qi: (b, N_HEADS + h)),
                  pl.BlockSpec((seq, V_DIM), lambda b, h, qi: (b, 2 * N_HEADS + h)),
                  pl.BlockSpec((tq, V_DIM), lambda b, h, qi: (b * nq + qi, sza0 + h)),
                  pl.BlockSpec((1, V_DIM), lambda b, h, qi: (0, 0))],
        out_specs=pl.BlockSpec((tq, V_DIM), lambda b, h, qi: (b * nq + qi, h)),
        out_shape=jax.ShapeDtypeStruct((m, N_HEADS * V_DIM), BF16),
        scratch_shapes=[pltpu.VMEM((2, tq, seq), F32)],
        compiler_params=_params(("parallel", "parallel", "arbitrary")),
        name="attn_prompt",
    )(lam_vecs, qkv16, qkv16, qkv16, aux, g_subln.reshape(1, V_DIM))


def _attn_decode_kernel(pt_ref, lv_ref, q_ref, kn_ref, vn_ref, sz_ref, g_ref, ck_hbm, cv_hbm, o_ref,
                        kbuf, vbuf, ksem, vsem, s_ref, p_ref, pn_ref, acc_ref,
                        *, pps, nchunk, n_new, page_base, lam_init, scale):
    b = pl.program_id(0)
    j = pl.program_id(1)
    nstep = 2 * nchunk
    g = b * nstep + j
    total = pl.num_programs(0) * nstep
    nrow = n_new * N_HEADS
    ncol = vbuf.shape[2]

    def page_copies(gg, apply):
        gg = jnp.asarray(gg, jnp.int32)
        bb = gg // nstep
        jj = gg % nstep

        @pl.when(jj < nchunk)
        def _():
            slot = (bb * nchunk + jj) % PAGE_SLOTS
            for i in range(pps):
                page = page_base + pt_ref[bb, jj * pps + i]
                apply(pltpu.make_async_copy(ck_hbm.at[page], kbuf.at[slot, i], ksem.at[slot, i]))

        @pl.when(jj >= nchunk)
        def _():
            c = jj - nchunk
            slot = (bb * nchunk + c) % PAGE_SLOTS
            for i in range(pps):
                page = page_base + pt_ref[bb, c * pps + i]
                apply(pltpu.make_async_copy(cv_hbm.at[page], vbuf.at[slot, i], vsem.at[slot, i]))

    @pl.when(g == 0)
    def _():
        for d in range(PAGE_LOOKAHEAD):
            page_copies(d, lambda cp: cp.start())

    @pl.when(g + PAGE_LOOKAHEAD < total)
    def _():
        page_copies(g + PAGE_LOOKAHEAD, lambda cp: cp.start())

    page_copies(g, lambda cp: cp.wait())
    slot = (b * nchunk + jnp.where(j < nchunk, j, j - nchunk)) % PAGE_SLOTS
    k_refs = [kbuf.at[slot, i] for i in range(pps)]
    v_refs = [vbuf.at[slot, i] for i in range(pps)]

    @pl.when(j < nchunk)
    def _():
        qq = q_ref[...].reshape(nrow, V_DIM)
        rows = lax.broadcasted_iota(jnp.int32, (nrow, ncol), 0)
        cols = lax.broadcasted_iota(jnp.int32, (nrow, ncol), 1)
        valid = (cols & (N_HEADS - 1)) == (rows & (N_HEADS - 1))
        for mp in range(2):
            qm = qq[:, mp * HEAD_DIM:(mp + 1) * HEAD_DIM].astype(BF16)
            for i in range(pps):
                km = k_refs[i][pl.ds(mp, ncol, stride=2), :].astype(BF16)
                s = _dot_nt(qm, km) * scale
                s_ref[j, mp, :, i * ncol:(i + 1) * ncol] = jnp.where(valid, s, -jnp.inf)

    @pl.when(j == nchunk - 1)
    def _():
        lam = _lambda(lv_ref[...], lam_init)
        qq = q_ref[...].reshape(nrow, V_DIM)
        lane = lax.broadcasted_iota(jnp.int32, (nrow, LANES), 1)
        qrow = lax.broadcasted_iota(jnp.int32, (nrow, LANES), 0) >> 3
        sn = [jnp.full((nrow, LANES), -jnp.inf, F32) for _ in range(2)]
        for t in range(n_new):
            ktile = jnp.concatenate([kn_ref[t]] * n_new, axis=0)
            prod = qq * ktile
            for mp in range(2):
                col = jnp.sum(prod[:, mp * HEAD_DIM:(mp + 1) * HEAD_DIM], axis=1, keepdims=True) * scale
                sn[mp] = jnp.where((lane == t) & (qrow >= t), col, sn[mp])
        pn = []
        rinv = []
        for mp in range(2):
            mx = jnp.max(sn[mp], axis=1, keepdims=True)
            for c in range(nchunk):
                mx = jnp.maximum(mx, jnp.max(s_ref[c, mp], axis=1, keepdims=True))
            pnew = jnp.exp(sn[mp] - mx)
            l = jnp.sum(pnew, axis=1, keepdims=True)
            for c in range(nchunk):
                p = jnp.exp(s_ref[c, mp] - mx)
                s_ref[c, mp] = p
                l = l + jnp.sum(p, axis=1, keepdims=True)
            pn.append(pnew)
            rinv.append(1.0 / l)
        r1 = rinv[0]
        r2 = lam * rinv[1]
        for c in range(nchunk):
            p_ref[c] = (s_ref[c, 0] * r1 - s_ref[c, 1] * r2).astype(BF16)
        pn_ref[...] = (pn[0] * r1 - pn[1] * r2).astype(BF16).astype(F32)

    @pl.when(j == nchunk)
    def _():
        acc_ref[...] = jnp.zeros_like(acc_ref)

    @pl.when(j >= nchunk)
    def _():
        c = j - nchunk
        a = None
        for i in range(pps):
            vp = v_refs[i][...].astype(BF16)
            d = _dot(p_ref[c, :, i * ncol:(i + 1) * ncol], vp)
            a = d if a is None else a + d
        acc_ref[...] += a

    @pl.when(j == 2 * nchunk - 1)
    def _():
        o = acc_ref[...]
        pnew = pn_ref[...]
        for t in range(n_new):
            vtile = jnp.concatenate([vn_ref[t]] * n_new, axis=0)
            o = o + pnew[:, t:t + 1] * vtile
        y = _subln_gate(o, g_ref[...], sz_ref[...].reshape(nrow, V_DIM), lam_init)
        o_ref[...] = y.astype(o_ref.dtype).reshape(n_new, N_HEADS, V_DIM)


def _attn_decode(page_table, lam_vecs, qkv_s5, aux_s4, g_subln, ck, cv, page_base, lam_init, pps=4):
    n_new, dec_b = qkv_s5.shape[:2]
    n_pages = page_table.shape[1]
    nchunk = n_pages // pps
    nrow = n_new * N_HEADS
    ncol = cv.shape[1]
    sza0 = (2 * POOL_WIDTH) // (N_HEADS * V_DIM)

    def small(sel):
        return pl.BlockSpec((n_new, None, None, N_HEADS, V_DIM), lambda b, j, pt: (0, b, sel, 0, 0))

    kern = functools.partial(_attn_decode_kernel, pps=pps, nchunk=nchunk, n_new=n_new, page_base=page_base,
                             lam_init=lam_init, scale=HEAD_DIM ** -0.5)
    grid_spec = pltpu.PrefetchScalarGridSpec(
        num_scalar_prefetch=1,
        grid=(dec_b, 2 * nchunk),
        in_specs=[pl.BlockSpec((4, HEAD_DIM), lambda b, j, pt: (0, 0)),
                  small(0), small(1), small(2),
                  pl.BlockSpec((n_new, None, N_HEADS, V_DIM), lambda b, j, pt: (0, b, sza0, 0)),
                  pl.BlockSpec((1, V_DIM), lambda b, j, pt: (0, 0)),
                  pl.BlockSpec(memory_space=pl.ANY),
                  pl.BlockSpec(memory_space=pl.ANY)],
        out_specs=pl.BlockSpec((n_new, None, N_HEADS, V_DIM), lambda b, j, pt: (0, b, 0, 0)),
        scratch_shapes=[pltpu.VMEM((PAGE_SLOTS, pps, 2 * ncol, HEAD_DIM), F32),
                        pltpu.VMEM((PAGE_SLOTS, pps, ncol, V_DIM), F32),
                        pltpu.SemaphoreType.DMA((PAGE_SLOTS, pps)),
                        pltpu.SemaphoreType.DMA((PAGE_SLOTS, pps)),
                        pltpu.VMEM((nchunk, 2, nrow, pps * ncol), F32),
                        pltpu.VMEM((nchunk, nrow, pps * ncol), BF16),
                        pltpu.VMEM((nrow, LANES), F32),
                        pltpu.VMEM((nrow, V_DIM), F32)],
    )
    return pl.pallas_call(
        kern,
        grid_spec=grid_spec,
        out_shape=jax.ShapeDtypeStruct((n_new, dec_b, N_HEADS, V_DIM), F32),
        compiler_params=_params(("arbitrary", "arbitrary")),
        name="attn_decode",
    )(page_table, lam_vecs, qkv_s5, qkv_s5, qkv_s5, aux_s4, g_subln.reshape(1, V_DIM), ck, cv)


def _merge_kernel(po_ref, ao_ref, wpp_ref, wpa_ref, gp_ref, ga_ref, o_ref):
    pp = _dot(po_ref[...].astype(BF16), wpp_ref[...].astype(BF16))
    pa = _dot(ao_ref[...].astype(BF16), wpa_ref[...].astype(BF16))
    o_ref[...] = (gp_ref[...] * pp + ga_ref[...] * pa).astype(BF16)


def _merge(pool_out, attn_out, aux, w_proj_pool, w_proj_attn, tm, tn=512):
    m = pool_out.shape[0]
    gp0 = (2 * POOL_WIDTH + D_MODEL) // tn
    ga0 = gp0 + D_MODEL // tn
    return pl.pallas_call(
        _merge_kernel,
        grid=(m // tm, D_MODEL // tn),
        in_specs=[pl.BlockSpec((tm, POOL_WIDTH), lambda i, j: (i, 0)),
                  pl.BlockSpec((tm, D_MODEL), lambda i, j: (i, 0)),
                  pl.BlockSpec((POOL_WIDTH, tn), lambda i, j: (0, j)),
                  pl.BlockSpec((D_MODEL, tn), lambda i, j: (0, j)),
                  pl.BlockSpec((tm, tn), lambda i, j: (i, gp0 + j)),
                  pl.BlockSpec((tm, tn), lambda i, j: (i, ga0 + j))],
        out_specs=pl.BlockSpec((tm, tn), lambda i, j: (i, j)),
        out_shape=jax.ShapeDtypeStruct((m, D_MODEL), BF16),
        compiler_params=_params(("parallel", "arbitrary")),
        name="gated_merge",
    )(pool_out, attn_out, w_proj_pool, w_proj_attn, aux, aux)


def _out_kernel(m_ref, w_ref, x_ref, gate_ref, g_ref, y_ref, acc_ref, *, tn, nj):
    j = pl.program_id(1)
    acc_ref[j] = _dot(m_ref[...], w_ref[...].astype(BF16))

    @pl.when(j == nj - 1)
    def _():
        ss = None
        for c in range(nj):
            a = acc_ref[c]
            t = jnp.sum(a * a, axis=-1, keepdims=True)
            ss = t if ss is None else ss + t
        r = lax.rsqrt(ss / float(nj * tn) + EPS)
        for c in range(nj):
            sl = slice(c * tn, (c + 1) * tn)
            y_ref[:, sl] = x_ref[:, sl] + gate_ref[:, sl] * (acc_ref[c] * r * g_ref[:, sl])


def _out_proj(merged, w_out, x2d, mod, mod_spec2, g_post, tm, tn=512):
    m, d = x2d.shape
    nj = d // tn
    return pl.pallas_call(
        functools.partial(_out_kernel, tn=tn, nj=nj),
        grid=(m // tm, nj),
        in_specs=[pl.BlockSpec((tm, d), lambda i, j: (i, 0)),
                  pl.BlockSpec((d, tn), lambda i, j: (0, j)),
                  pl.BlockSpec((tm, d), lambda i, j: (i, 0)),
                  mod_spec2(2),
                  pl.BlockSpec((1, d), lambda i, j: (0, 0))],
        out_specs=pl.BlockSpec((tm, d), lambda i, j: (i, 0)),
        out_shape=jax.ShapeDtypeStruct((m, d), F32),
        scratch_shapes=[pltpu.VMEM((nj, tm, tn), F32)],
        compiler_params=_params(("parallel", "arbitrary")),
        name="out_proj",
    )(merged, w_out, x2d, mod, g_post.reshape(1, d))


def _rope_tables(pos):
    half = HEAD_DIM // 2
    inv_freq = ROPE_THETA ** (-jnp.arange(half, dtype=F32) / half)
    ang = pos.astype(F32)[:, None] * inv_freq[None, :]
    cos = jnp.cos(ang)
    sin = jnp.sin(ang)
    return jnp.concatenate([cos, cos], axis=1), jnp.concatenate([-sin, sin], axis=1)


def kernel(x_prompt, x_sample, cache_k, cache_v, state_pool, page_table, c_prompt, c_sample,
           w_ada, b_ada, g_pre, g_post, w_in, w_pool_grp, pool_scale,
           lambda_q1, lambda_k1, lambda_q2, lambda_k2, g_subln, w_proj_pool, w_proj_attn, w_out):
    batch, seq, d = x_prompt.shape
    dec_b, n_new, _ = x_sample.shape
    depth = w_in.shape[0]
    n_pages = page_table.shape[1]
    n_pool_pages, page_size = cache_k.shape[1], cache_k.shape[2]
    past_len = n_pages * page_size
    width = N_HEADS * V_DIM
    ck = cache_k.reshape(depth * n_pool_pages, page_size * N_HEADS * 2, HEAD_DIM)
    cv = cache_v.reshape(depth * n_pool_pages, page_size * N_HEADS, V_DIM)

    cos_p, sin_p = _rope_tables(jnp.arange(seq))
    cos_s, sin_s = _rope_tables(past_len + jnp.repeat(jnp.arange(n_new), dec_b))

    xp = x_prompt.reshape(batch * seq, d)
    xs = x_sample.transpose(1, 0, 2).reshape(n_new * dec_b, d)
    c_all = jnp.concatenate([c_prompt, c_sample], axis=0)
    tm_p = 1024
    tiles_per_seq = seq // tm_p

    outs = {name: [] for name in ("kp", "vp", "up", "ks", "vs", "us")}
    for l in range(depth):
        lam_init = 0.8 - 0.6 * math.exp(-0.3 * l)
        lam_vecs = jnp.stack([lambda_q1[l], lambda_k1[l], lambda_q2[l], lambda_k2[l]])
        mod = _mod(c_all, w_ada[l], b_ada[l])
        mod_p = mod[:batch].reshape(batch, 1, 3 * d)
        mod_s = mod[batch:]

        def spec_p(tm):
            per_seq = seq // tm
            return lambda col: pl.BlockSpec((None, 1, d), lambda i, *_: (i // per_seq, 0, col))

        def spec_s(col):
            return pl.BlockSpec((dec_b, d), lambda i, *_: (0, col))

        h = _modulated_norm(xp, g_pre[l], mod_p, spec_p(512), 512)
        aux = _aux_proj(h, w_in[l], tm_p)
        qkv16, k32, v32 = _qkv_proj(h, w_in[l], cos_p, sin_p, tm_p, BF16)
        pool_out = _pool_prompt(aux, w_pool_grp[l], pool_scale[l], seq)
        attn_out = _attn_prompt(lam_vecs, qkv16, aux, g_subln[l], batch, seq, lam_init)
        merged = _merge(pool_out, attn_out, aux, w_proj_pool[l], w_proj_attn[l], tm_p)
        xp_new = _out_proj(merged, w_out[l], xp, mod_p, spec_p(512), g_post[l], 512)
        outs["kp"].append(k32.reshape(batch, seq, N_HEADS, 2, HEAD_DIM))
        outs["vp"].append(v32.reshape(batch, seq, N_HEADS, V_DIM))
        outs["up"].append(aux.reshape(batch, seq, AUX_WIDTH)[:, seq - POOL_BUF:, :POOL_WIDTH])

        ms = n_new * dec_b
        h_s = _modulated_norm(xs, g_pre[l], mod_s, spec_s, dec_b)
        aux_s = _aux_proj(h_s, w_in[l], ms)
        qkv_s, k32_s, v32_s = _qkv_proj(h_s, w_in[l], cos_s, sin_s, ms, F32)
        sp_t = state_pool[l].transpose(1, 0, 2)
        pool_out_s = _pool_sample(sp_t, aux_s.reshape(n_new, dec_b, AUX_WIDTH), w_pool_grp[l], pool_scale[l])
        attn_out_s = _attn_decode(
            page_table, lam_vecs,
            qkv_s.reshape(n_new, dec_b, 3, N_HEADS, V_DIM),
            aux_s.reshape(n_new, dec_b, AUX_WIDTH // V_DIM, V_DIM),
            g_subln[l], ck, cv, l * n_pool_pages, lam_init)
        merged_s = _merge(pool_out_s, attn_out_s.reshape(ms, width), aux_s,
                          w_proj_pool[l], w_proj_attn[l], ms)
        xs_new = _out_proj(merged_s, w_out[l], xs, mod_s, spec_s, g_post[l], dec_b)

        def to_bt(a):
            return a.reshape(n_new, dec_b, -1).transpose(1, 0, 2)

        outs["ks"].append(to_bt(k32_s).reshape(dec_b, n_new, N_HEADS, 2, HEAD_DIM))
        outs["vs"].append(to_bt(v32_s).reshape(dec_b, n_new, N_HEADS, V_DIM))
        u_ext = jnp.concatenate([state_pool[l], to_bt(aux_s[:, :POOL_WIDTH])], axis=1)
        outs["us"].append(u_ext[:, -POOL_BUF:])
        xp, xs = xp_new, xs_new

    yp = xp.reshape(batch, seq, d)
    ys = xs.reshape(n_new, dec_b, d).transpose(1, 0, 2)
    return (yp, ys, jnp.stack(outs["kp"]), jnp.stack(outs["vp"]), jnp.stack(outs["up"]),
            jnp.stack(outs["ks"]), jnp.stack(outs["vs"]), jnp.stack(outs["us"]))
```

```python
import functools
import math

import jax
import jax.numpy as jnp
from jax import lax
from jax.experimental import pallas as pl
from jax.experimental.pallas import tpu as pltpu

F32 = jnp.float32
BF16 = jnp.bfloat16

EPS = 1e-6
ROPE_THETA = 10000.0
N_HEADS = 8
HEAD_DIM = 128
V_DIM = 2 * HEAD_DIM
D_MODEL = 2048
POOL_WIDTH = 1024
POOL_GROUP = 256
POOL_WINDOWS = (2, 4, 8, 16)
POOL_BUF = 15
LANES = 128
COL_CHUNK = 256
PAGE_LOOKAHEAD = 3
PAGE_SLOTS = PAGE_LOOKAHEAD + 1
HALO = 16
AUX_WIDTH = 8192
VMEM_LIMIT = 56 * 1024 * 1024


def _params(sem):
    return pltpu.CompilerParams(dimension_semantics=sem, vmem_limit_bytes=VMEM_LIMIT)


def _dot(a, b):
    return jnp.dot(a, b, preferred_element_type=F32)


def _dot_nt(a, b):
    return lax.dot_general(a, b, (((1,), (1,)), ((), ())), preferred_element_type=F32)


def _sigmoid(x):
    return jax.nn.sigmoid(x)


def _mod_kernel(c_ref, w_ref, b_ref, o_ref):
    c = c_ref[...]
    a = (c * _sigmoid(c)).astype(BF16)
    o_ref[...] = _dot(a, w_ref[...].astype(BF16)) + b_ref[...]


def _mod(c_all, w_ada, b_ada, tn=512):
    m, d = c_all.shape
    n = w_ada.shape[1]
    return pl.pallas_call(
        _mod_kernel,
        grid=(n // tn,),
        in_specs=[pl.BlockSpec((m, d), lambda j: (0, 0)),
                  pl.BlockSpec((d, tn), lambda j: (0, j)),
                  pl.BlockSpec((1, tn), lambda j: (0, j))],
        out_specs=pl.BlockSpec((m, tn), lambda j: (0, j)),
        out_shape=jax.ShapeDtypeStruct((m, n), F32),
        compiler_params=_params(("parallel",)),
        name="adaln_mod",
    )(c_all, w_ada, b_ada.reshape(1, n))


def _h_kernel(x_ref, g_ref, sc_ref, sh_ref, o_ref):
    x = x_ref[...]
    ms = jnp.mean(x * x, axis=-1, keepdims=True)
    y = x * lax.rsqrt(ms + EPS) * g_ref[...]
    o_ref[...] = (y * (1.0 + sc_ref[...]) + sh_ref[...]).astype(BF16)


def _modulated_norm(x2d, g_pre, mod, mod_spec, tm):
    m, d = x2d.shape
    return pl.pallas_call(
        _h_kernel,
        grid=(m // tm,),
        in_specs=[pl.BlockSpec((tm, d), lambda i: (i, 0)),
                  pl.BlockSpec((1, d), lambda i: (0, 0)),
                  mod_spec(1), mod_spec(0)],
        out_specs=pl.BlockSpec((tm, d), lambda i: (i, 0)),
        out_shape=jax.ShapeDtypeStruct((m, d), BF16),
        compiler_params=_params(("parallel",)),
        name="modulated_norm",
    )(x2d, g_pre.reshape(1, d), mod, mod)


def _col_chunks(tn):
    return [(c * COL_CHUNK, COL_CHUNK) for c in range(tn // COL_CHUNK)]


def _aux_kernel(h_ref, w_ref, o_ref, *, tn):
    j = pl.program_id(1)
    n_plain = POOL_WIDTH // tn
    n_silu_end = (2 * POOL_WIDTH + D_MODEL) // tn

    def run(epilogue):
        for c0, cw in _col_chunks(tn):
            acc = _dot(h_ref[...], w_ref[:, c0:c0 + cw].astype(BF16))
            o_ref[:, c0:c0 + cw] = epilogue(acc)

    @pl.when(j < n_plain)
    def _():
        run(lambda a: a)

    @pl.when((j >= n_plain) & (j < n_silu_end))
    def _():
        run(lambda a: a * _sigmoid(a))

    @pl.when(j >= n_silu_end)
    def _():
        run(_sigmoid)


def _aux_proj(h, w_in, tm, tn=1024):
    m, d = h.shape
    n_head = (2 * POOL_WIDTH) // tn
    skip = (3 * D_MODEL) // tn
    return pl.pallas_call(
        functools.partial(_aux_kernel, tn=tn),
        grid=(m // tm, AUX_WIDTH // tn),
        in_specs=[pl.BlockSpec((tm, d), lambda i, j: (i, 0)),
                  pl.BlockSpec((d, tn), lambda i, j: (0, jnp.where(j < n_head, j, j + skip)))],
        out_specs=pl.BlockSpec((tm, tn), lambda i, j: (i, j)),
        out_shape=jax.ShapeDtypeStruct((m, AUX_WIDTH), F32),
        compiler_params=_params(("parallel", "arbitrary")),
        name="aux_proj",
    )(h, w_in)


def _rope(acc, cos, sin_signed):
    parts = []
    for c in range(acc.shape[1] // LANES):
        xs = acc[:, c * LANES:(c + 1) * LANES]
        parts.append(xs * cos + pltpu.roll(xs, HEAD_DIM // 2, axis=1) * sin_signed)
    return jnp.concatenate(parts, axis=1)


def _qkv_kernel(h_ref, w_ref, cos_ref, sin_ref, lo_ref, k_ref, v_ref, *, tm, tn):
    j = pl.program_id(1)
    nb = D_MODEL // tn
    n_hm = 2 * N_HEADS

    def chunk(c0, cw):
        return _dot(h_ref[...], w_ref[:, c0:c0 + cw].astype(BF16))

    @pl.when(j < nb)
    def _():
        for c0, cw in _col_chunks(tn):
            r = _rope(chunk(c0, cw), cos_ref[...], sin_ref[...])
            lo_ref[:, c0:c0 + cw] = r.astype(BF16).astype(lo_ref.dtype)

    @pl.when((j >= nb) & (j < 2 * nb))
    def _():
        for c0, cw in _col_chunks(tn):
            r = _rope(chunk(c0, cw), cos_ref[...], sin_ref[...])
            lo_ref[:, c0:c0 + cw] = r.astype(BF16).astype(lo_ref.dtype)
            for e in range(cw // LANES):
                hm = (j - nb) * (tn // LANES) + c0 // LANES + e
                k_ref[pl.ds(hm, tm, stride=n_hm), :] = r[:, e * LANES:(e + 1) * LANES]

    @pl.when(j >= 2 * nb)
    def _():
        for c0, cw in _col_chunks(tn):
            acc = chunk(c0, cw)
            v_ref[:, c0:c0 + cw] = acc
            lo_ref[:, c0:c0 + cw] = acc.astype(BF16).astype(lo_ref.dtype)


def _qkv_proj(h, w_in, cos2, sin2, tm, lo_dtype, tn=512):
    m, d = h.shape
    nb = D_MODEL // tn
    n_hm = 2 * N_HEADS
    col0 = (2 * POOL_WIDTH) // tn
    n_pos = cos2.shape[0] // tm
    return pl.pallas_call(
        functools.partial(_qkv_kernel, tm=tm, tn=tn),
        grid=(m // tm, 3 * nb),
        in_specs=[pl.BlockSpec((tm, d), lambda i, j: (i, 0)),
                  pl.BlockSpec((d, tn), lambda i, j: (0, col0 + j)),
                  pl.BlockSpec((tm, LANES), lambda i, j: (i % n_pos, 0)),
                  pl.BlockSpec((tm, LANES), lambda i, j: (i % n_pos, 0))],
        out_specs=[pl.BlockSpec((tm, tn), lambda i, j: (i, j)),
                   pl.BlockSpec((tm * n_hm, HEAD_DIM), lambda i, j: (i, 0)),
                   pl.BlockSpec((tm, tn), lambda i, j: (i, jnp.clip(j - 2 * nb, 0, nb - 1)))],
        out_shape=[jax.ShapeDtypeStruct((m, 3 * D_MODEL), lo_dtype),
                   jax.ShapeDtypeStruct((m * n_hm, HEAD_DIM), F32),
                   jax.ShapeDtypeStruct((m, D_MODEL), F32)],
        compiler_params=_params(("parallel", "arbitrary")),
        name="qkv_proj",
    )(h, w_in, cos2, sin2)


def _pool_mix(pooled_groups, wg_ref, scale_ref, szp):
    mixed = [_dot(p.astype(BF16), wg_ref[g].astype(BF16)) for g, p in enumerate(pooled_groups)]
    mixed = jnp.concatenate(mixed, axis=1)
    return (mixed * scale_ref[...] * szp).astype(BF16)


def _pool_prompt_kernel(u_ref, halo_ref, szp_ref, wg_ref, scale_ref, o_ref, *, tp, tiles_per_seq):
    i = pl.program_id(0)
    first = (i % tiles_per_seq) == 0
    halo = jnp.where(first, 0.0, halo_ref[...])
    ext = jnp.concatenate([halo, u_ref[...]], axis=0)
    pos = (i % tiles_per_seq) * tp + lax.broadcasted_iota(jnp.int32, (tp, 1), 0)
    groups = []
    for g, w in enumerate(POOL_WINDOWS):
        e = ext[:, g * POOL_GROUP:(g + 1) * POOL_GROUP]
        a = e
        s = 1
        while s < w:
            a = a + pltpu.roll(a, s, axis=0)
            s *= 2
        cnt = jnp.minimum(pos + 1, w).astype(F32)
        groups.append(a[HALO:] / cnt - e[HALO:])
    o_ref[...] = _pool_mix(groups, wg_ref, scale_ref, szp_ref[...])


def _pool_prompt(aux, w_pool_grp, pool_scale, seq, tp=512):
    m = aux.shape[0]
    tiles_per_seq = seq // tp
    hb = tp // HALO
    return pl.pallas_call(
        functools.partial(_pool_prompt_kernel, tp=tp, tiles_per_seq=tiles_per_seq),
        grid=(m // tp,),
        in_specs=[pl.BlockSpec((tp, POOL_WIDTH), lambda i: (i, 0)),
                  pl.BlockSpec((HALO, POOL_WIDTH), lambda i: (jnp.maximum(i * hb - 1, 0), 0)),
                  pl.BlockSpec((tp, POOL_WIDTH), lambda i: (i, 1)),
                  pl.BlockSpec((4, POOL_GROUP, POOL_GROUP), lambda i: (0, 0, 0)),
                  pl.BlockSpec((1, POOL_WIDTH), lambda i: (0, 0))],
        out_specs=pl.BlockSpec((tp, POOL_WIDTH), lambda i: (i, 0)),
        out_shape=jax.ShapeDtypeStruct((m, POOL_WIDTH), BF16),
        compiler_params=_params(("parallel",)),
        name="pool_prompt",
    )(aux, aux, aux, w_pool_grp, pool_scale.reshape(1, POOL_WIDTH))


def _pool_sample_kernel(sp_ref, u_ref, szp_ref, wg_ref, scale_ref, o_ref, *, n_new):
    nb = u_ref.shape[1]
    ext = [sp_ref[r] for r in range(POOL_BUF)] + [u_ref[t] for t in range(n_new)]
    groups = []
    for g, w in enumerate(POOL_WINDOWS):
        per_t = []
        for t in range(n_new):
            end = POOL_BUF + t
            acc = ext[end][:, g * POOL_GROUP:(g + 1) * POOL_GROUP]
            cur = acc
            for r in range(end - w + 1, end):
                acc = acc + ext[r][:, g * POOL_GROUP:(g + 1) * POOL_GROUP]
            per_t.append(acc / float(w) - cur)
        groups.append(jnp.concatenate(per_t, axis=0))
    szp = szp_ref[...].reshape(n_new * nb, POOL_WIDTH)
    o_ref[...] = _pool_mix(groups, wg_ref, scale_ref, szp)


def _pool_sample(sp_t, aux_s3, w_pool_grp, pool_scale):
    n_new, nb, _ = aux_s3.shape
    return pl.pallas_call(
        functools.partial(_pool_sample_kernel, n_new=n_new),
        grid=(1,),
        in_specs=[pl.BlockSpec((POOL_BUF, nb, POOL_WIDTH), lambda i: (0, 0, 0)),
                  pl.BlockSpec((n_new, nb, POOL_WIDTH), lambda i: (0, 0, 0)),
                  pl.BlockSpec((n_new, nb, POOL_WIDTH), lambda i: (0, 0, 1)),
                  pl.BlockSpec((4, POOL_GROUP, POOL_GROUP), lambda i: (0, 0, 0)),
                  pl.BlockSpec((1, POOL_WIDTH), lambda i: (0, 0))],
        out_specs=pl.BlockSpec((n_new * nb, POOL_WIDTH), lambda i: (0, 0)),
        out_shape=jax.ShapeDtypeStruct((n_new * nb, POOL_WIDTH), BF16),
        compiler_params=_params(("arbitrary",)),
        name="pool_sample",
    )(sp_t, aux_s3, aux_s3, w_pool_grp, pool_scale.reshape(1, POOL_WIDTH))


def _lambda(lv, lam_init):
    l1 = jnp.sum(lv[0:1] * lv[1:2], axis=1, keepdims=True)
    l2 = jnp.sum(lv[2:3] * lv[3:4], axis=1, keepdims=True)
    return jnp.exp(l1) - jnp.exp(l2) + lam_init


def _subln_gate(o, g, sz, lam_init):
    ms = jnp.mean(o * o, axis=-1, keepdims=True)
    y = o * lax.rsqrt(ms + EPS) * g
    return (y * (1.0 - lam_init) * sz).astype(BF16)


def _attn_prompt_kernel(lv_ref, q_ref, k_ref, v_ref, sz_ref, g_ref, o_ref, s_ref,
                        *, tq, nq, lam_init, scale):
    qi = pl.program_id(2)
    lam = _lambda(lv_ref[...], lam_init)
    c_exp = scale * math.log2(math.e)

    def fold(x, op):
        r = x[:, :LANES]
        for c in range(1, tq // LANES):
            r = op(r, x[:, c * LANES:(c + 1) * LANES])
        return r

    def tile(n):
        row = lax.broadcasted_iota(jnp.int32, (tq, tq), 0)
        col = lax.broadcasted_iota(jnp.int32, (tq, tq), 1)
        pv, rinv = [], []
        q = [q_ref[:, mp * HEAD_DIM:(mp + 1) * HEAD_DIM] for mp in range(2)]
        m_run = [None, None]
        for kc in range(n + 1):
            for mp in range(2):
                s = _dot_nt(q[mp], k_ref[kc * tq:(kc + 1) * tq, mp * HEAD_DIM:(mp + 1) * HEAD_DIM]) * c_exp
                if kc == n:
                    s = jnp.where(col <= row, s, -jnp.inf)
                s_ref[mp, :, kc * tq:(kc + 1) * tq] = s
                f = fold(s, jnp.maximum)
                m_run[mp] = f if m_run[mp] is None else jnp.maximum(m_run[mp], f)
        for mp in range(2):
            mx = jnp.max(m_run[mp], axis=1, keepdims=True)
            l_run = None
            acc = None
            for kc in range(n + 1):
                p = jnp.exp2(s_ref[mp, :, kc * tq:(kc + 1) * tq] - mx)
                f = fold(p, jnp.add)
                l_run = f if l_run is None else l_run + f
                d = _dot(p.astype(BF16), v_ref[kc * tq:(kc + 1) * tq, :])
                acc = d if acc is None else acc + d
            rinv.append(1.0 / jnp.sum(l_run, axis=1, keepdims=True))
            pv.append(acc)
        o = pv[0] * rinv[0] - pv[1] * (lam * rinv[1])
        o_ref[...] = _subln_gate(o, g_ref[...], sz_ref[...], lam_init)

    for n in range(nq):
        pl.when(qi == n)(functools.partial(tile, n))


def _attn_prompt(lam_vecs, qkv16, aux, g_subln, batch, seq, lam_init, tq=512):
    m = qkv16.shape[0]
    nq = seq // tq
    sza0 = (2 * POOL_WIDTH) // V_DIM
    kern = functools.partial(_attn_prompt_kernel, tq=tq, nq=nq, lam_init=lam_init, scale=HEAD_DIM ** -0.5)
    return pl.pallas_call(
        kern,
        grid=(batch, N_HEADS, nq),
        in_specs=[pl.BlockSpec((4, HEAD_DIM), lambda b, h, qi: (0, 0)),
                  pl.BlockSpec((tq, V_DIM), lambda b, h, qi: (b * nq + qi, h)),
                  pl.BlockSpec((seq, V_DIM), lambda b, h, qi: (b, N_HEADS + h)),
                  pl.BlockSpec((seq, V_DIM), lambda b, h, qi: (b, 2 * N_HEADS + h)),
                  pl.BlockSpec((tq, V_DIM), lambda b, h, qi: (b * nq + qi, sza0 + h)),
                  pl.BlockSpec((1, V_DIM), lambda b, h, qi: (0, 0))],
        out_specs=pl.BlockSpec((tq, V_DIM), lambda b, h, qi: (b * nq + qi, h)),
        out_shape=jax.ShapeDtypeStruct((m, N_HEADS * V_DIM), BF16),
        scratch_shapes=[pltpu.VMEM((2, tq, seq), F32)],
        compiler_params=_params(("parallel", "parallel", "arbitrary")),
        name="attn_prompt",
    )(lam_vecs, qkv16, qkv16, qkv16, aux, g_subln.reshape(1, V_DIM))


def _attn_decode_kernel(pt_ref, lv_ref, q_ref, kn_ref, vn_ref, sz_ref, g_ref, ck_hbm, cv_hbm, o_ref,
                        kbuf, vbuf, ksem, vsem, s_ref, p_ref, pn_ref, acc_ref,
                        *, pps, nchunk, n_new, page_base, lam_init, scale):
    b = pl.program_id(0)
    j = pl.program_id(1)
    nstep = 2 * nchunk
    g = b * nstep + j
    total = pl.num_programs(0) * nstep
    nrow = n_new * N_HEADS
    ncol = vbuf.shape[2]

    def page_copies(gg, apply):
        gg = jnp.asarray(gg, jnp.int32)
        bb = gg // nstep
        jj = gg % nstep

        @pl.when(jj < nchunk)
        def _():
            slot = (bb * nchunk + jj) % PAGE_SLOTS
            for i in range(pps):
                page = page_base + pt_ref[bb, jj * pps + i]
                apply(pltpu.make_async_copy(ck_hbm.at[page], kbuf.at[slot, i], ksem.at[slot, i]))

        @pl.when(jj >= nchunk)
        def _():
            c = jj - nchunk
            slot = (bb * nchunk + c) % PAGE_SLOTS
            for i in range(pps):
                page = page_base + pt_ref[bb, c * pps + i]
                apply(pltpu.make_async_copy(cv_hbm.at[page], vbuf.at[slot, i], vsem.at[slot, i]))

    @pl.when(g == 0)
    def _():
        for d in range(PAGE_LOOKAHEAD):
            page_copies(d, lambda cp: cp.start())

    @pl.when(g + PAGE_LOOKAHEAD < total)
    def _():
        page_copies(g + PAGE_LOOKAHEAD, lambda cp: cp.start())

    page_copies(g, lambda cp: cp.wait())
    slot = (b * nchunk + jnp.where(j < nchunk, j, j - nchunk)) % PAGE_SLOTS
    k_refs = [kbuf.at[slot, i] for i in range(pps)]
    v_refs = [vbuf.at[slot, i] for i in range(pps)]

    @pl.when(j < nchunk)
    def _():
        qq = q_ref[...].reshape(nrow, V_DIM)
        rows = lax.broadcasted_iota(jnp.int32, (nrow, ncol), 0)
        cols = lax.broadcasted_iota(jnp.int32, (nrow, ncol), 1)
        valid = (cols & (N_HEADS - 1)) == (rows & (N_HEADS - 1))
        for mp in range(2):
            qm = qq[:, mp * HEAD_DIM:(mp + 1) * HEAD_DIM].astype(BF16)
            for i in range(pps):
                km = k_refs[i][pl.ds(mp, ncol, stride=2), :].astype(BF16)
                s = _dot_nt(qm, km) * scale
                s_ref[j, mp, :, i * ncol:(i + 1) * ncol] = jnp.where(valid, s, -jnp.inf)

    @pl.when(j == nchunk - 1)
    def _():
        lam = _lambda(lv_ref[...], lam_init)
        qq = q_ref[...].reshape(nrow, V_DIM)
        lane = lax.broadcasted_iota(jnp.int32, (nrow, LANES), 1)
        qrow = lax.broadcasted_iota(jnp.int32, (nrow, LANES), 0) >> 3
        sn = [jnp.full((nrow, LANES), -jnp.inf, F32) for _ in range(2)]
        for t in range(n_new):
            ktile = jnp.concatenate([kn_ref[t]] * n_new, axis=0)
            prod = qq * ktile
            for mp in range(2):
                col = jnp.sum(prod[:, mp * HEAD_DIM:(mp + 1) * HEAD_DIM], axis=1, keepdims=True) * scale
                sn[mp] = jnp.where((lane == t) & (qrow >= t), col, sn[mp])
        pn = []
        rinv = []
        for mp in range(2):
            mx = jnp.max(sn[mp], axis=1, keepdims=True)
            for c in range(nchunk):
                mx = jnp.maximum(mx, jnp.max(s_ref[c, mp], axis=1, keepdims=True))
            pnew = jnp.exp(sn[mp] - mx)
            l = jnp.sum(pnew, axis=1, keepdims=True)
            for c in range(nchunk):
                p = jnp.exp(s_ref[c, mp] - mx)
                s_ref[c, mp] = p
                l = l + jnp.sum(p, axis=1, keepdims=True)
            pn.append(pnew)
            rinv.append(1.0 / l)
        r1 = rinv[0]
        r2 = lam * rinv[1]
        for c in range(nchunk):
            p_ref[c] = (s_ref[c, 0] * r1 - s_ref[c, 1] * r2).astype(BF16)
        pn_ref[...] = (pn[0] * r1 - pn[1] * r2).astype(BF16).astype(F32)

    @pl.when(j == nchunk)
    def _():
        acc_ref[...] = jnp.zeros_like(acc_ref)

    @pl.when(j >= nchunk)
    def _():
        c = j - nchunk
        a = None
        for i in range(pps):
            vp = v_refs[i][...].astype(BF16)
            d = _dot(p_ref[c, :, i * ncol:(i + 1) * ncol], vp)
            a = d if a is None else a + d
        acc_ref[...] += a

    @pl.when(j == 2 * nchunk - 1)
    def _():
        o = acc_ref[...]
        pnew = pn_ref[...]
        for t in range(n_new):
            vtile = jnp.concatenate([vn_ref[t]] * n_new, axis=0)
            o = o + pnew[:, t:t + 1] * vtile
        y = _subln_gate(o, g_ref[...], sz_ref[...].reshape(nrow, V_DIM), lam_init)
        o_ref[...] = y.astype(o_ref.dtype).reshape(n_new, N_HEADS, V_DIM)


def _attn_decode(page_table, lam_vecs, qkv_s5, aux_s4, g_subln, ck, cv, page_base, lam_init, pps=4):
    n_new, dec_b = qkv_s5.shape[:2]
    n_pages = page_table.shape[1]
    nchunk = n_pages // pps
    nrow = n_new * N_HEADS
    ncol = cv.shape[1]
    sza0 = (2 * POOL_WIDTH) // (N_HEADS * V_DIM)

    def small(sel):
        return pl.BlockSpec((n_new, None, None, N_HEADS, V_DIM), lambda b, j, pt: (0, b, sel, 0, 0))

    kern = functools.partial(_attn_decode_kernel, pps=pps, nchunk=nchunk, n_new=n_new, page_base=page_base,
                             lam_init=lam_init, scale=HEAD_DIM ** -0.5)
    grid_spec = pltpu.PrefetchScalarGridSpec(
        num_scalar_prefetch=1,
        grid=(dec_b, 2 * nchunk),
        in_specs=[pl.BlockSpec((4, HEAD_DIM), lambda b, j, pt: (0, 0)),
                  small(0), small(1), small(2),
                  pl.BlockSpec((n_new, None, N_HEADS, V_DIM), lambda b, j, pt: (0, b, sza0, 0)),
                  pl.BlockSpec((1, V_DIM), lambda b, j, pt: (0, 0)),
                  pl.BlockSpec(memory_space=pl.ANY),
                  pl.BlockSpec(memory_space=pl.ANY)],
        out_specs=pl.BlockSpec((n_new, None, N_HEADS, V_DIM), lambda b, j, pt: (0, b, 0, 0)),
        scratch_shapes=[pltpu.VMEM((PAGE_SLOTS, pps, 2 * ncol, HEAD_DIM), F32),
                        pltpu.VMEM((PAGE_SLOTS, pps, ncol, V_DIM), F32),
                        pltpu.SemaphoreType.DMA((PAGE_SLOTS, pps)),
                        pltpu.SemaphoreType.DMA((PAGE_SLOTS, pps)),
                        pltpu.VMEM((nchunk, 2, nrow, pps * ncol), F32),
                        pltpu.VMEM((nchunk, nrow, pps * ncol), BF16),
                        pltpu.VMEM((nrow, LANES), F32),
                        pltpu.VMEM((nrow, V_DIM), F32)],
    )
    return pl.pallas_call(
        kern,
        grid_spec=grid_spec,
        out_shape=jax.ShapeDtypeStruct((n_new, dec_b, N_HEADS, V_DIM), F32),
        compiler_params=_params(("arbitrary", "arbitrary")),
        name="attn_decode",
    )(page_table, lam_vecs, qkv_s5, qkv_s5, qkv_s5, aux_s4, g_subln.reshape(1, V_DIM), ck, cv)


def _merge_kernel(po_ref, ao_ref, wpp_ref, wpa_ref, gp_ref, ga_ref, o_ref):
    pp = _dot(po_ref[...].astype(BF16), wpp_ref[...].astype(BF16))
    pa = _dot(ao_ref[...].astype(BF16), wpa_ref[...].astype(BF16))
    o_ref[...] = (gp_ref[...] * pp + ga_ref[...] * pa).astype(BF16)


def _merge(pool_out, attn_out, aux, w_proj_pool, w_proj_attn, tm, tn=512):
    m = pool_out.shape[0]
    gp0 = (2 * POOL_WIDTH + D_MODEL) // tn
    ga0 = gp0 + D_MODEL // tn
    return pl.pallas_call(
        _merge_kernel,
        grid=(m // tm, D_MODEL // tn),
        in_specs=[pl.BlockSpec((tm, POOL_WIDTH), lambda i, j: (i, 0)),
                  pl.BlockSpec((tm, D_MODEL), lambda i, j: (i, 0)),
                  pl.BlockSpec((POOL_WIDTH, tn), lambda i, j: (0, j)),
                  pl.BlockSpec((D_MODEL, tn), lambda i, j: (0, j)),
                  pl.BlockSpec((tm, tn), lambda i, j: (i, gp0 + j)),
                  pl.BlockSpec((tm, tn), lambda i, j: (i, ga0 + j))],
        out_specs=pl.BlockSpec((tm, tn), lambda i, j: (i, j)),
        out_shape=jax.ShapeDtypeStruct((m, D_MODEL), BF16),
        compiler_params=_params(("parallel", "arbitrary")),
        name="gated_merge",
    )(pool_out, attn_out, w_proj_pool, w_proj_attn, aux, aux)


def _out_kernel(m_ref, w_ref, x_ref, gate_ref, g_ref, y_ref, acc_ref, *, tn, nj):
    j = pl.program_id(1)
    acc_ref[j] = _dot(m_ref[...], w_ref[...].astype(BF16))

    @pl.when(j == nj - 1)
    def _():
        ss = None
        for c in range(nj):
            a = acc_ref[c]
            t = jnp.sum(a * a, axis=-1, keepdims=True)
            ss = t if ss is None else ss + t
        r = lax.rsqrt(ss / float(nj * tn) + EPS)
        for c in range(nj):
            sl = slice(c * tn, (c + 1) * tn)
            y_ref[:, sl] = x_ref[:, sl] + gate_ref[:, sl] * (acc_ref[c] * r * g_ref[:, sl])


def _out_proj(merged, w_out, x2d, mod, mod_spec2, g_post, tm, tn=512):
    m, d = x2d.shape
    nj = d // tn
    return pl.pallas_call(
        functools.partial(_out_kernel, tn=tn, nj=nj),
        grid=(m // tm, nj),
        in_specs=[pl.BlockSpec((tm, d), lambda i, j: (i, 0)),
                  pl.BlockSpec((d, tn), lambda i, j: (0, j)),
                  pl.BlockSpec((tm, d), lambda i, j: (i, 0)),
                  mod_spec2(2),
                  pl.BlockSpec((1, d), lambda i, j: (0, 0))],
        out_specs=pl.BlockSpec((tm, d), lambda i, j: (i, 0)),
        out_shape=jax.ShapeDtypeStruct((m, d), F32),
        scratch_shapes=[pltpu.VMEM((nj, tm, tn), F32)],
        compiler_params=_params(("parallel", "arbitrary")),
        name="out_proj",
    )(merged, w_out, x2d, mod, g_post.reshape(1, d))


def _rope_tables(pos):
    half = HEAD_DIM // 2
    inv_freq = ROPE_THETA ** (-jnp.arange(half, dtype=F32) / half)
    ang = pos.astype(F32)[:, None] * inv_freq[None, :]
    cos = jnp.cos(ang)
    sin = jnp.sin(ang)
    return jnp.concatenate([cos, cos], axis=1), jnp.concatenate([-sin, sin], axis=1)


def kernel(x_prompt, x_sample, cache_k, cache_v, state_pool, page_table, c_prompt, c_sample,
           w_ada, b_ada, g_pre, g_post, w_in, w_pool_grp, pool_scale,
           lambda_q1, lambda_k1, lambda_q2, lambda_k2, g_subln, w_proj_pool, w_proj_attn, w_out):
    batch, seq, d = x_prompt.shape
    dec_b, n_new, _ = x_sample.shape
    depth = w_in.shape[0]
    n_pages = page_table.shape[1]
    n_pool_pages, page_size = cache_k.shape[1], cache_k.shape[2]
    past_len = n_pages * page_size
    width = N_HEADS * V_DIM
    ck = cache_k.reshape(depth * n_pool_pages, page_size * N_HEADS * 2, HEAD_DIM)
    cv = cache_v.reshape(depth * n_pool_pages, page_size * N_HEADS, V_DIM)

    cos_p, sin_p = _rope_tables(jnp.arange(seq))
    cos_s, sin_s = _rope_tables(past_len + jnp.repeat(jnp.arange(n_new), dec_b))

    xp = x_prompt.reshape(batch * seq, d)
    xs = x_sample.transpose(1, 0, 2).reshape(n_new * dec_b, d)
    c_all = jnp.concatenate([c_prompt, c_sample], axis=0)
    tm_p = 1024
    tiles_per_seq = seq // tm_p

    outs = {name: [] for name in ("kp", "vp", "up", "ks", "vs", "us")}
    for l in range(depth):
        lam_init = 0.8 - 0.6 * math.exp(-0.3 * l)
        lam_vecs = jnp.stack([lambda_q1[l], lambda_k1[l], lambda_q2[l], lambda_k2[l]])
        mod = _mod(c_all, w_ada[l], b_ada[l])
        mod_p = mod[:batch].reshape(batch, 1, 3 * d)
        mod_s = mod[batch:]

        def spec_p(tm):
            per_seq = seq // tm
            return lambda col: pl.BlockSpec((None, 1, d), lambda i, *_: (i // per_seq, 0, col))

        def spec_s(col):
            return pl.BlockSpec((dec_b, d), lambda i, *_: (0, col))

        h = _modulated_norm(xp, g_pre[l], mod_p, spec_p(512), 512)
        aux = _aux_proj(h, w_in[l], tm_p)
        qkv16, k32, v32 = _qkv_proj(h, w_in[l], cos_p, sin_p, tm_p, BF16)
        pool_out = _pool_prompt(aux, w_pool_grp[l], pool_scale[l], seq)
        attn_out = _attn_prompt(lam_vecs, qkv16, aux, g_subln[l], batch, seq, lam_init)
        merged = _merge(pool_out, attn_out, aux, w_proj_pool[l], w_proj_attn[l], tm_p)
        xp_new = _out_proj(merged, w_out[l], xp, mod_p, spec_p(512), g_post[l], 512)
        outs["kp"].append(k32.reshape(batch, seq, N_HEADS, 2, HEAD_DIM))
        outs["vp"].append(v32.reshape(batch, seq, N_HEADS, V_DIM))
        outs["up"].append(aux.reshape(batch, seq, AUX_WIDTH)[:, seq - POOL_BUF:, :POOL_WIDTH])

        ms = n_new * dec_b
        h_s = _modulated_norm(xs, g_pre[l], mod_s, spec_s, dec_b)
        aux_s = _aux_proj(h_s, w_in[l], ms)
        qkv_s, k32_s, v32_s = _qkv_proj(h_s, w_in[l], cos_s, sin_s, ms, F32)
        sp_t = state_pool[l].transpose(1, 0, 2)
        pool_out_s = _pool_sample(sp_t, aux_s.reshape(n_new, dec_b, AUX_WIDTH), w_pool_grp[l], pool_scale[l])
        attn_out_s = _attn_decode(
            page_table, lam_vecs,
            qkv_s.reshape(n_new, dec_b, 3, N_HEADS, V_DIM),
            aux_s.reshape(n_new, dec_b, AUX_WIDTH // V_DIM, V_DIM),
            g_subln[l], ck, cv, l * n_pool_pages, lam_init)
        merged_s = _merge(pool_out_s, attn_out_s.reshape(ms, width), aux_s,
                          w_proj_pool[l], w_proj_attn[l], ms)
        xs_new = _out_proj(merged_s, w_out[l], xs, mod_s, spec_s, g_post[l], dec_b)

        def to_bt(a):
            return a.reshape(n_new, dec_b, -1).transpose(1, 0, 2)

        outs["ks"].append(to_bt(k32_s).reshape(dec_b, n_new, N_HEADS, 2, HEAD_DIM))
        outs["vs"].append(to_bt(v32_s).reshape(dec_b, n_new, N_HEADS, V_DIM))
        u_ext = jnp.concatenate([state_pool[l], to_bt(aux_s[:, :POOL_WIDTH])], axis=1)
        outs["us"].append(u_ext[:, -POOL_BUF:])
        xp, xs = xp_new, xs_new

    yp = xp.reshape(batch, seq, d)
    ys = xs.reshape(n_new, dec_b, d).transpose(1, 0, 2)
    return (yp, ys, jnp.stack(outs["kp"]), jnp.stack(outs["vp"]), jnp.stack(outs["up"]),
            jnp.stack(outs["ks"]), jnp.stack(outs["vs"]), jnp.stack(outs["us"]))
```

```python
import functools
import math

import jax
import jax.numpy as jnp
from jax import lax
from jax.experimental import pallas as pl
from jax.experimental.pallas import tpu as pltpu

F32 = jnp.float32
BF16 = jnp.bfloat16

EPS = 1e-6
ROPE_THETA = 10000.0
N_HEADS = 8
HEAD_DIM = 128
V_DIM = 2 * HEAD_DIM
D_MODEL = 2048
POOL_WIDTH = 1024
POOL_GROUP = 256
POOL_WINDOWS = (2, 4, 8, 16)
POOL_BUF = 15
LANES = 128
COL_CHUNK = 256
TM_PROJ = 1024
TM_ROWWISE = 512
TN_ADALN = 2048
PAGE_LOOKAHEAD = 3
PAGE_SLOTS = PAGE_LOOKAHEAD + 1
HALO = 16
AUX_WIDTH = 8192
VMEM_LIMIT = 56 * 1024 * 1024


def _params(sem):
    return pltpu.CompilerParams(dimension_semantics=sem, vmem_limit_bytes=VMEM_LIMIT)


def _dot(a, b):
    return jnp.dot(a, b, preferred_element_type=F32)


def _dot_nt(a, b):
    return lax.dot_general(a, b, (((1,), (1,)), ((), ())), preferred_element_type=F32)


def _sigmoid(x):
    return jax.nn.sigmoid(x)


def _mod_kernel(c_ref, w_ref, b_ref, o_ref):
    c = c_ref[...]
    a = (c * _sigmoid(c)).astype(BF16)
    o_ref[...] = _dot(a, w_ref[...].astype(BF16)) + b_ref[...]


def _mod(c_all, w_ada, b_ada, tn=512):
    m, d = c_all.shape
    n = w_ada.shape[1]
    return pl.pallas_call(
        _mod_kernel,
        grid=(n // tn,),
        in_specs=[pl.BlockSpec((m, d), lambda j: (0, 0)),
                  pl.BlockSpec((d, tn), lambda j: (0, j)),
                  pl.BlockSpec((1, tn), lambda j: (0, j))],
        out_specs=pl.BlockSpec((m, tn), lambda j: (0, j)),
        out_shape=jax.ShapeDtypeStruct((m, n), F32),
        compiler_params=_params(("parallel",)),
        name="adaln_mod",
    )(c_all, w_ada, b_ada.reshape(1, n))


def _h_kernel(x_ref, g_ref, sc_ref, sh_ref, o_ref):
    x = x_ref[...]
    ms = jnp.mean(x * x, axis=-1, keepdims=True)
    y = x * lax.rsqrt(ms + EPS) * g_ref[...]
    o_ref[...] = (y * (1.0 + sc_ref[...]) + sh_ref[...]).astype(BF16)


def _modulated_norm(x2d, g_pre, mod, mod_spec, tm):
    m, d = x2d.shape
    return pl.pallas_call(
        _h_kernel,
        grid=(m // tm,),
        in_specs=[pl.BlockSpec((tm, d), lambda i: (i, 0)),
                  pl.BlockSpec((1, d), lambda i: (0, 0)),
                  mod_spec(1), mod_spec(0)],
        out_specs=pl.BlockSpec((tm, d), lambda i: (i, 0)),
        out_shape=jax.ShapeDtypeStruct((m, d), BF16),
        compiler_params=_params(("parallel",)),
        name="modulated_norm",
    )(x2d, g_pre.reshape(1, d), mod, mod)


def _col_chunks(tn):
    return [(c * COL_CHUNK, COL_CHUNK) for c in range(tn // COL_CHUNK)]


def _aux_kernel(h_ref, w_ref, o_ref, *, tn):
    j = pl.program_id(1)
    n_plain = POOL_WIDTH // tn
    n_silu_end = (2 * POOL_WIDTH + D_MODEL) // tn

    def run(epilogue):
        for c0, cw in _col_chunks(tn):
            acc = _dot(h_ref[...], w_ref[:, c0:c0 + cw].astype(BF16))
            o_ref[:, c0:c0 + cw] = epilogue(acc)

    @pl.when(j < n_plain)
    def _():
        run(lambda a: a)

    @pl.when((j >= n_plain) & (j < n_silu_end))
    def _():
        run(lambda a: a * _sigmoid(a))

    @pl.when(j >= n_silu_end)
    def _():
        run(_sigmoid)


def _aux_proj(h, w_in, tm, tn=1024):
    m, d = h.shape
    n_head = (2 * POOL_WIDTH) // tn
    skip = (3 * D_MODEL) // tn
    return pl.pallas_call(
        functools.partial(_aux_kernel, tn=tn),
        grid=(m // tm, AUX_WIDTH // tn),
        in_specs=[pl.BlockSpec((tm, d), lambda i, j: (i, 0)),
                  pl.BlockSpec((d, tn), lambda i, j: (0, jnp.where(j < n_head, j, j + skip)))],
        out_specs=pl.BlockSpec((tm, tn), lambda i, j: (i, j)),
        out_shape=jax.ShapeDtypeStruct((m, AUX_WIDTH), F32),
        compiler_params=_params(("parallel", "arbitrary")),
        name="aux_proj",
    )(h, w_in)


def _rope(acc, cos, sin_signed):
    parts = []
    for c in range(acc.shape[1] // LANES):
        xs = acc[:, c * LANES:(c + 1) * LANES]
        parts.append(xs * cos + pltpu.roll(xs, HEAD_DIM // 2, axis=1) * sin_signed)
    return jnp.concatenate(parts, axis=1)


def _qkv_kernel(h_ref, w_ref, cos_ref, sin_ref, lo_ref, k_ref, v_ref, *, tm, tn):
    j = pl.program_id(1)
    nb = D_MODEL // tn
    n_hm = 2 * N_HEADS

    def chunk(c0, cw):
        return _dot(h_ref[...], w_ref[:, c0:c0 + cw].astype(BF16))

    @pl.when(j < nb)
    def _():
        for c0, cw in _col_chunks(tn):
            r = _rope(chunk(c0, cw), cos_ref[...], sin_ref[...])
            lo_ref[:, c0:c0 + cw] = r.astype(BF16).astype(lo_ref.dtype)

    @pl.when((j >= nb) & (j < 2 * nb))
    def _():
        for c0, cw in _col_chunks(tn):
            r = _rope(chunk(c0, cw), cos_ref[...], sin_ref[...])
            lo_ref[:, c0:c0 + cw] = r.astype(BF16).astype(lo_ref.dtype)
            for e in range(cw // LANES):
                hm = (j - nb) * (tn // LANES) + c0 // LANES + e
                k_ref[pl.ds(hm, tm, stride=n_hm), :] = r[:, e * LANES:(e + 1) * LANES]

    @pl.when(j >= 2 * nb)
    def _():
        for c0, cw in _col_chunks(tn):
            acc = chunk(c0, cw)
            v_ref[:, c0:c0 + cw] = acc
            lo_ref[:, c0:c0 + cw] = acc.astype(BF16).astype(lo_ref.dtype)


def _qkv_proj(h, w_in, cos2, sin2, tm, lo_dtype, tn=512):
    m, d = h.shape
    nb = D_MODEL // tn
    n_hm = 2 * N_HEADS
    col0 = (2 * POOL_WIDTH) // tn
    n_pos = cos2.shape[0] // tm
    return pl.pallas_call(
        functools.partial(_qkv_kernel, tm=tm, tn=tn),
        grid=(m // tm, 3 * nb),
        in_specs=[pl.BlockSpec((tm, d), lambda i, j: (i, 0)),
                  pl.BlockSpec((d, tn), lambda i, j: (0, col0 + j)),
                  pl.BlockSpec((tm, LANES), lambda i, j: (i % n_pos, 0)),
                  pl.BlockSpec((tm, LANES), lambda i, j: (i % n_pos, 0))],
        out_specs=[pl.BlockSpec((tm, tn), lambda i, j: (i, j)),
                   pl.BlockSpec((tm * n_hm, HEAD_DIM), lambda i, j: (i, 0)),
                   pl.BlockSpec((tm, tn), lambda i, j: (i, jnp.clip(j - 2 * nb, 0, nb - 1)))],
        out_shape=[jax.ShapeDtypeStruct((m, 3 * D_MODEL), lo_dtype),
                   jax.ShapeDtypeStruct((m * n_hm, HEAD_DIM), F32),
                   jax.ShapeDtypeStruct((m, D_MODEL), F32)],
        compiler_params=_params(("parallel", "arbitrary")),
        name="qkv_proj",
    )(h, w_in, cos2, sin2)


def _pool_mix(pooled_groups, wg_ref, scale_ref, szp):
    mixed = [_dot(p.astype(BF16), wg_ref[g].astype(BF16)) for g, p in enumerate(pooled_groups)]
    mixed = jnp.concatenate(mixed, axis=1)
    return (mixed * scale_ref[...] * szp).astype(BF16)


def _pool_prompt_kernel(u_ref, halo_ref, szp_ref, wg_ref, scale_ref, o_ref, *, tp, tiles_per_seq):
    i = pl.program_id(0)
    first = (i % tiles_per_seq) == 0
    halo = jnp.where(first, 0.0, halo_ref[...])
    ext = jnp.concatenate([halo, u_ref[...]], axis=0)
    pos = (i % tiles_per_seq) * tp + lax.broadcasted_iota(jnp.int32, (tp, 1), 0)
    groups = []
    for g, w in enumerate(POOL_WINDOWS):
        e = ext[:, g * POOL_GROUP:(g + 1) * POOL_GROUP]
        a = e
        s = 1
        while s < w:
            a = a + pltpu.roll(a, s, axis=0)
            s *= 2
        cnt = jnp.minimum(pos + 1, w).astype(F32)
        groups.append(a[HALO:] / cnt - e[HALO:])
    o_ref[...] = _pool_mix(groups, wg_ref, scale_ref, szp_ref[...])


def _pool_prompt(aux, w_pool_grp, pool_scale, seq, tp=512):
    m = aux.shape[0]
    tiles_per_seq = seq // tp
    hb = tp // HALO
    return pl.pallas_call(
        functools.partial(_pool_prompt_kernel, tp=tp, tiles_per_seq=tiles_per_seq),
        grid=(m // tp,),
        in_specs=[pl.BlockSpec((tp, POOL_WIDTH), lambda i: (i, 0)),
                  pl.BlockSpec((HALO, POOL_WIDTH), lambda i: (jnp.maximum(i * hb - 1, 0), 0)),
                  pl.BlockSpec((tp, POOL_WIDTH), lambda i: (i, 1)),
                  pl.BlockSpec((4, POOL_GROUP, POOL_GROUP), lambda i: (0, 0, 0)),
                  pl.BlockSpec((1, POOL_WIDTH), lambda i: (0, 0))],
        out_specs=pl.BlockSpec((tp, POOL_WIDTH), lambda i: (i, 0)),
        out_shape=jax.ShapeDtypeStruct((m, POOL_WIDTH), BF16),
        compiler_params=_params(("parallel",)),
        name="pool_prompt",
    )(aux, aux, aux, w_pool_grp, pool_scale.reshape(1, POOL_WIDTH))


def _pool_sample_kernel(sp_ref, u_ref, szp_ref, wg_ref, scale_ref, o_ref, *, n_new):
    nb = u_ref.shape[1]
    ext = [sp_ref[r] for r in range(POOL_BUF)] + [u_ref[t] for t in range(n_new)]
    groups = []
    for g, w in enumerate(POOL_WINDOWS):
        per_t = []
        for t in range(n_new):
            end = POOL_BUF + t
            acc = ext[end][:, g * POOL_GROUP:(g + 1) * POOL_GROUP]
            cur = acc
            for r in range(end - w + 1, end):
                acc = acc + ext[r][:, g * POOL_GROUP:(g + 1) * POOL_GROUP]
            per_t.append(acc / float(w) - cur)
        groups.append(jnp.concatenate(per_t, axis=0))
    szp = szp_ref[...].reshape(n_new * nb, POOL_WIDTH)
    o_ref[...] = _pool_mix(groups, wg_ref, scale_ref, szp)


def _pool_sample(sp_t, aux_s3, w_pool_grp, pool_scale):
    n_new, nb, _ = aux_s3.shape
    return pl.pallas_call(
        functools.partial(_pool_sample_kernel, n_new=n_new),
        grid=(1,),
        in_specs=[pl.BlockSpec((POOL_BUF, nb, POOL_WIDTH), lambda i: (0, 0, 0)),
                  pl.BlockSpec((n_new, nb, POOL_WIDTH), lambda i: (0, 0, 0)),
                  pl.BlockSpec((n_new, nb, POOL_WIDTH), lambda i: (0, 0, 1)),
                  pl.BlockSpec((4, POOL_GROUP, POOL_GROUP), lambda i: (0, 0, 0)),
                  pl.BlockSpec((1, POOL_WIDTH), lambda i: (0, 0))],
        out_specs=pl.BlockSpec((n_new * nb, POOL_WIDTH), lambda i: (0, 0)),
        out_shape=jax.ShapeDtypeStruct((n_new * nb, POOL_WIDTH), BF16),
        compiler_params=_params(("arbitrary",)),
        name="pool_sample",
    )(sp_t, aux_s3, aux_s3, w_pool_grp, pool_scale.reshape(1, POOL_WIDTH))


def _lambda(lv, lam_init):
    l1 = jnp.sum(lv[0:1] * lv[1:2], axis=1, keepdims=True)
    l2 = jnp.sum(lv[2:3] * lv[3:4], axis=1, keepdims=True)
    return jnp.exp(l1) - jnp.exp(l2) + lam_init


def _subln_gate(o, g, sz, lam_init):
    ms = jnp.mean(o * o, axis=-1, keepdims=True)
    y = o * lax.rsqrt(ms + EPS) * g
    return (y * (1.0 - lam_init) * sz).astype(BF16)


def _attn_prompt_kernel(lv_ref, q_ref, k_ref, v_ref, sz_ref, g_ref, o_ref, s_ref,
                        *, tq, nq, lam_init, scale):
    qi = pl.program_id(2)
    lam = _lambda(lv_ref[...], lam_init)
    c_exp = scale * math.log2(math.e)

    def fold(x, op):
        r = x[:, :LANES]
        for c in range(1, tq // LANES):
            r = op(r, x[:, c * LANES:(c + 1) * LANES])
        return r

    def tile(n):
        row = lax.broadcasted_iota(jnp.int32, (tq, tq), 0)
        col = lax.broadcasted_iota(jnp.int32, (tq, tq), 1)
        pv, rinv = [], []
        q = [q_ref[:, mp * HEAD_DIM:(mp + 1) * HEAD_DIM] for mp in range(2)]
        m_run = [None, None]
        for kc in range(n + 1):
            for mp in range(2):
                s = _dot_nt(q[mp], k_ref[kc * tq:(kc + 1) * tq, mp * HEAD_DIM:(mp + 1) * HEAD_DIM]) * c_exp
                if kc == n:
                    s = jnp.where(col <= row, s, -jnp.inf)
                s_ref[mp, :, kc * tq:(kc + 1) * tq] = s
                f = fold(s, jnp.maximum)
                m_run[mp] = f if m_run[mp] is None else jnp.maximum(m_run[mp], f)
        for mp in range(2):
            mx = jnp.max(m_run[mp], axis=1, keepdims=True)
            l_run = None
            acc = None
            for kc in range(n + 1):
                p = jnp.exp2(s_ref[mp, :, kc * tq:(kc + 1) * tq] - mx)
                f = fold(p, jnp.add)
                l_run = f if l_run is None else l_run + f
                d = _dot(p.astype(BF16), v_ref[kc * tq:(kc + 1) * tq, :])
                acc = d if acc is None else acc + d
            rinv.append(1.0 / jnp.sum(l_run, axis=1, keepdims=True))
            pv.append(acc)
        o = pv[0] * rinv[0] - pv[1] * (lam * rinv[1])
        o_ref[...] = _subln_gate(o, g_ref[...], sz_ref[...], lam_init)

    for n in range(nq):
        pl.when(qi == n)(functools.partial(tile, n))


def _attn_prompt(lam_vecs, qkv16, aux, g_subln, batch, seq, lam_init, tq=512):
    m = qkv16.shape[0]
    nq = seq // tq
    sza0 = (2 * POOL_WIDTH) // V_DIM
    kern = functools.partial(_attn_prompt_kernel, tq=tq, nq=nq, lam_init=lam_init, scale=HEAD_DIM ** -0.5)
    return pl.pallas_call(
        kern,
        grid=(batch, N_HEADS, nq),
        in_specs=[pl.BlockSpec((4, HEAD_DIM), lambda b, h, qi: (0, 0)),
                  pl.BlockSpec((tq, V_DIM), lambda b, h, qi: (b * nq + qi, h)),
                  pl.BlockSpec((seq, V_DIM), lambda b, h, qi: (b, N_HEADS + h)),
                  pl.BlockSpec((seq, V_DIM), lambda b, h, qi: (b, 2 * N_HEADS + h)),
                  pl.BlockSpec((tq, V_DIM), lambda b, h, qi: (b * nq + qi, sza0 + h)),
                  pl.BlockSpec((1, V_DIM), lambda b, h, qi: (0, 0))],
        out_specs=pl.BlockSpec((tq, V_DIM), lambda b, h, qi: (b * nq + qi, h)),
        out_shape=jax.ShapeDtypeStruct((m, N_HEADS * V_DIM), BF16),
        scratch_shapes=[pltpu.VMEM((2, tq, seq), F32)],
        compiler_params=_params(("parallel", "parallel", "arbitrary")),
        name="attn_prompt",
    )(lam_vecs, qkv16, qkv16, qkv16, aux, g_subln.reshape(1, V_DIM))


def _attn_decode_kernel(pt_ref, lv_ref, q_ref, kn_ref, vn_ref, sz_ref, g_ref, ck_hbm, cv_hbm, o_ref,
                        kbuf, vbuf, ksem, vsem, s_ref, p_ref, pn_ref, acc_ref,
                        *, pps, nchunk, n_new, page_base, lam_init, scale):
    b = pl.program_id(0)
    j = pl.program_id(1)
    nstep = 2 * nchunk
    g = b * nstep + j
    total = pl.num_programs(0) * nstep
    nrow = n_new * N_HEADS
    ncol = vbuf.shape[2]

    def page_copies(gg, apply):
        gg = jnp.asarray(gg, jnp.int32)
        bb = gg // nstep
        jj = gg % nstep

        @pl.when(jj < nchunk)
        def _():
            slot = (bb * nchunk + jj) % PAGE_SLOTS
            for i in range(pps):
                page = page_base + pt_ref[bb, jj * pps + i]
                apply(pltpu.make_async_copy(ck_hbm.at[page], kbuf.at[slot, i], ksem.at[slot, i]))

        @pl.when(jj >= nchunk)
        def _():
            c = jj - nchunk
            slot = (bb * nchunk + c) % PAGE_SLOTS
            for i in range(pps):
                page = page_base + pt_ref[bb, c * pps + i]
                apply(pltpu.make_async_copy(cv_hbm.at[page], vbuf.at[slot, i], vsem.at[slot, i]))

    @pl.when(g == 0)
    def _():
        for d in range(PAGE_LOOKAHEAD):
            page_copies(d, lambda cp: cp.start())

    @pl.when(g + PAGE_LOOKAHEAD < total)
    def _():
        page_copies(g + PAGE_LOOKAHEAD, lambda cp: cp.start())

    page_copies(g, lambda cp: cp.wait())
    slot = (b * nchunk + jnp.where(j < nchunk, j, j - nchunk)) % PAGE_SLOTS
    k_refs = [kbuf.at[slot, i] for i in range(pps)]
    v_refs = [vbuf.at[slot, i] for i in range(pps)]

    @pl.when(j < nchunk)
    def _():
        qq = q_ref[...].reshape(nrow, V_DIM)
        rows = lax.broadcasted_iota(jnp.int32, (nrow, ncol), 0)
        cols = lax.broadcasted_iota(jnp.int32, (nrow, ncol), 1)
        valid = (cols & (N_HEADS - 1)) == (rows & (N_HEADS - 1))
        for mp in range(2):
            qm = qq[:, mp * HEAD_DIM:(mp + 1) * HEAD_DIM].astype(BF16)
            for i in range(pps):
                km = k_refs[i][pl.ds(mp, ncol, stride=2), :].astype(BF16)
                s = _dot_nt(qm, km) * scale
                s_ref[j, mp, :, i * ncol:(i + 1) * ncol] = jnp.where(valid, s, -jnp.inf)

    @pl.when(j == nchunk - 1)
    def _():
        lam = _lambda(lv_ref[...], lam_init)
        qq = q_ref[...].reshape(nrow, V_DIM)
        lane = lax.broadcasted_iota(jnp.int32, (nrow, LANES), 1)
        qrow = lax.broadcasted_iota(jnp.int32, (nrow, LANES), 0) >> 3
        sn = [jnp.full((nrow, LANES), -jnp.inf, F32) for _ in range(2)]
        for t in range(n_new):
            ktile = jnp.concatenate([kn_ref[t]] * n_new, axis=0)
            prod = qq * ktile
            for mp in range(2):
                col = jnp.sum(prod[:, mp * HEAD_DIM:(mp + 1) * HEAD_DIM], axis=1, keepdims=True) * scale
                sn[mp] = jnp.where((lane == t) & (qrow >= t), col, sn[mp])
        pn = []
        rinv = []
        for mp in range(2):
            mx = jnp.max(sn[mp], axis=1, keepdims=True)
            for c in range(nchunk):
                mx = jnp.maximum(mx, jnp.max(s_ref[c, mp], axis=1, keepdims=True))
            pnew = jnp.exp(sn[mp] - mx)
            l = jnp.sum(pnew, axis=1, keepdims=True)
            for c in range(nchunk):
                p = jnp.exp(s_ref[c, mp] - mx)
                s_ref[c, mp] = p
                l = l + jnp.sum(p, axis=1, keepdims=True)
            pn.append(pnew)
            rinv.append(1.0 / l)
        r1 = rinv[0]
        r2 = lam * rinv[1]
        for c in range(nchunk):
            p_ref[c] = (s_ref[c, 0] * r1 - s_ref[c, 1] * r2).astype(BF16)
        pn_ref[...] = (pn[0] * r1 - pn[1] * r2).astype(BF16).astype(F32)

    @pl.when(j == nchunk)
    def _():
        acc_ref[...] = jnp.zeros_like(acc_ref)

    @pl.when(j >= nchunk)
    def _():
        c = j - nchunk
        a = None
        for i in range(pps):
            vp = v_refs[i][...].astype(BF16)
            d = _dot(p_ref[c, :, i * ncol:(i + 1) * ncol], vp)
            a = d if a is None else a + d
        acc_ref[...] += a

    @pl.when(j == 2 * nchunk - 1)
    def _():
        o = acc_ref[...]
        pnew = pn_ref[...]
        for t in range(n_new):
            vtile = jnp.concatenate([vn_ref[t]] * n_new, axis=0)
            o = o + pnew[:, t:t + 1] * vtile
        y = _subln_gate(o, g_ref[...], sz_ref[...].reshape(nrow, V_DIM), lam_init)
        o_ref[...] = y.astype(o_ref.dtype).reshape(n_new, N_HEADS, V_DIM)


def _attn_decode(page_table, lam_vecs, qkv_s5, aux_s4, g_subln, ck, cv, page_base, lam_init, pps=4):
    n_new, dec_b = qkv_s5.shape[:2]
    n_pages = page_table.shape[1]
    nchunk = n_pages // pps
    nrow = n_new * N_HEADS
    ncol = cv.shape[1]
    sza0 = (2 * POOL_WIDTH) // (N_HEADS * V_DIM)

    def small(sel):
        return pl.BlockSpec((n_new, None, None, N_HEADS, V_DIM), lambda b, j, pt: (0, b, sel, 0, 0))

    kern = functools.partial(_attn_decode_kernel, pps=pps, nchunk=nchunk, n_new=n_new, page_base=page_base,
                             lam_init=lam_init, scale=HEAD_DIM ** -0.5)
    grid_spec = pltpu.PrefetchScalarGridSpec(
        num_scalar_prefetch=1,
        grid=(dec_b, 2 * nchunk),
        in_specs=[pl.BlockSpec((4, HEAD_DIM), lambda b, j, pt: (0, 0)),
                  small(0), small(1), small(2),
                  pl.BlockSpec((n_new, None, N_HEADS, V_DIM), lambda b, j, pt: (0, b, sza0, 0)),
                  pl.BlockSpec((1, V_DIM), lambda b, j, pt: (0, 0)),
                  pl.BlockSpec(memory_space=pl.ANY),
                  pl.BlockSpec(memory_space=pl.ANY)],
        out_specs=pl.BlockSpec((n_new, None, N_HEADS, V_DIM), lambda b, j, pt: (0, b, 0, 0)),
        scratch_shapes=[pltpu.VMEM((PAGE_SLOTS, pps, 2 * ncol, HEAD_DIM), F32),
                        pltpu.VMEM((PAGE_SLOTS, pps, ncol, V_DIM), F32),
                        pltpu.SemaphoreType.DMA((PAGE_SLOTS, pps)),
                        pltpu.SemaphoreType.DMA((PAGE_SLOTS, pps)),
                        pltpu.VMEM((nchunk, 2, nrow, pps * ncol), F32),
                        pltpu.VMEM((nchunk, nrow, pps * ncol), BF16),
                        pltpu.VMEM((nrow, LANES), F32),
                        pltpu.VMEM((nrow, V_DIM), F32)],
    )
    return pl.pallas_call(
        kern,
        grid_spec=grid_spec,
        out_shape=jax.ShapeDtypeStruct((n_new, dec_b, N_HEADS, V_DIM), F32),
        compiler_params=_params(("arbitrary", "arbitrary")),
        name="attn_decode",
    )(page_table, lam_vecs, qkv_s5, qkv_s5, qkv_s5, aux_s4, g_subln.reshape(1, V_DIM), ck, cv)


def _merge_kernel(po_ref, ao_ref, wpp_ref, wpa_ref, gp_ref, ga_ref, o_ref):
    pp = _dot(po_ref[...].astype(BF16), wpp_ref[...].astype(BF16))
    pa = _dot(ao_ref[...].astype(BF16), wpa_ref[...].astype(BF16))
    o_ref[...] = (gp_ref[...] * pp + ga_ref[...] * pa).astype(BF16)


def _merge(pool_out, attn_out, aux, w_proj_pool, w_proj_attn, tm, tn=512):
    m = pool_out.shape[0]
    gp0 = (2 * POOL_WIDTH + D_MODEL) // tn
    ga0 = gp0 + D_MODEL // tn
    return pl.pallas_call(
        _merge_kernel,
        grid=(m // tm, D_MODEL // tn),
        in_specs=[pl.BlockSpec((tm, POOL_WIDTH), lambda i, j: (i, 0)),
                  pl.BlockSpec((tm, D_MODEL), lambda i, j: (i, 0)),
                  pl.BlockSpec((POOL_WIDTH, tn), lambda i, j: (0, j)),
                  pl.BlockSpec((D_MODEL, tn), lambda i, j: (0, j)),
                  pl.BlockSpec((tm, tn), lambda i, j: (i, gp0 + j)),
                  pl.BlockSpec((tm, tn), lambda i, j: (i, ga0 + j))],
        out_specs=pl.BlockSpec((tm, tn), lambda i, j: (i, j)),
        out_shape=jax.ShapeDtypeStruct((m, D_MODEL), BF16),
        compiler_params=_params(("parallel", "arbitrary")),
        name="gated_merge",
    )(pool_out, attn_out, w_proj_pool, w_proj_attn, aux, aux)


def _out_kernel(m_ref, w_ref, x_ref, gate_ref, g_ref, y_ref, acc_ref, *, tn, nj):
    j = pl.program_id(1)
    acc_ref[j] = _dot(m_ref[...], w_ref[...].astype(BF16))

    @pl.when(j == nj - 1)
    def _():
        ss = None
        for c in range(nj):
            a = acc_ref[c]
            t = jnp.sum(a * a, axis=-1, keepdims=True)
            ss = t if ss is None else ss + t
        r = lax.rsqrt(ss / float(nj * tn) + EPS)
        tm, gate_rows = x_ref.shape[0], gate_ref.shape[0]
        for c in range(nj):
            sl = slice(c * tn, (c + 1) * tn)
            gate = gate_ref[:, sl]
            if gate_rows not in (1, tm):
                gate = jnp.concatenate([gate] * (tm // gate_rows), axis=0)
            y_ref[:, sl] = x_ref[:, sl] + gate * (acc_ref[c] * r * g_ref[:, sl])


def _out_proj(merged, w_out, x2d, mod, mod_spec2, g_post, tm, tn=512):
    m, d = x2d.shape
    nj = d // tn
    return pl.pallas_call(
        functools.partial(_out_kernel, tn=tn, nj=nj),
        grid=(m // tm, nj),
        in_specs=[pl.BlockSpec((tm, d), lambda i, j: (i, 0)),
                  pl.BlockSpec((d, tn), lambda i, j: (0, j)),
                  pl.BlockSpec((tm, d), lambda i, j: (i, 0)),
                  mod_spec2(2),
                  pl.BlockSpec((1, d), lambda i, j: (0, 0))],
        out_specs=pl.BlockSpec((tm, d), lambda i, j: (i, 0)),
        out_shape=jax.ShapeDtypeStruct((m, d), F32),
        scratch_shapes=[pltpu.VMEM((nj, tm, tn), F32)],
        compiler_params=_params(("parallel", "arbitrary")),
        name="out_proj",
    )(merged, w_out, x2d, mod, g_post.reshape(1, d))


def _rope_tables(pos):
    half = HEAD_DIM // 2
    inv_freq = ROPE_THETA ** (-jnp.arange(half, dtype=F32) / half)
    ang = pos.astype(F32)[:, None] * inv_freq[None, :]
    cos = jnp.cos(ang)
    sin = jnp.sin(ang)
    return jnp.concatenate([cos, cos], axis=1), jnp.concatenate([-sin, sin], axis=1)


def kernel(x_prompt, x_sample, cache_k, cache_v, state_pool, page_table, c_prompt, c_sample,
           w_ada, b_ada, g_pre, g_post, w_in, w_pool_grp, pool_scale,
           lambda_q1, lambda_k1, lambda_q2, lambda_k2, g_subln, w_proj_pool, w_proj_attn, w_out):
    batch, seq, d = x_prompt.shape
    dec_b, n_new, _ = x_sample.shape
    depth = w_in.shape[0]
    n_pages = page_table.shape[1]
    n_pool_pages, page_size = cache_k.shape[1], cache_k.shape[2]
    past_len = n_pages * page_size
    width = N_HEADS * V_DIM
    ck = cache_k.reshape(depth * n_pool_pages, page_size * N_HEADS * 2, HEAD_DIM)
    cv = cache_v.reshape(depth * n_pool_pages, page_size * N_HEADS, V_DIM)

    cos_p, sin_p = _rope_tables(jnp.arange(seq))
    cos_s, sin_s = _rope_tables(past_len + jnp.repeat(jnp.arange(n_new), dec_b))

    xp = x_prompt.reshape(batch * seq, d)
    xs = x_sample.transpose(1, 0, 2).reshape(n_new * dec_b, d)
    c_all = jnp.concatenate([c_prompt, c_sample], axis=0)
    tm_p = TM_PROJ

    outs = {name: [] for name in ("kp", "vp", "up", "ks", "vs", "us")}
    for l in range(depth):
        lam_init = 0.8 - 0.6 * math.exp(-0.3 * l)
        lam_vecs = jnp.stack([lambda_q1[l], lambda_k1[l], lambda_q2[l], lambda_k2[l]])
        mod = _mod(c_all, w_ada[l], b_ada[l], TN_ADALN)
        wpp, wpa, wout = (w.astype(BF16) for w in (w_proj_pool[l], w_proj_attn[l], w_out[l]))
        mod_p = mod[:batch].reshape(batch, 1, 3 * d)
        mod_s = mod[batch:]

        def spec_p(tm):
            per_seq = seq // tm
            return lambda col: pl.BlockSpec((None, 1, d), lambda i, *_: (i // per_seq, 0, col))

        def spec_s(col):
            return pl.BlockSpec((dec_b, d), lambda i, *_: (0, col))

        h = _modulated_norm(xp, g_pre[l], mod_p, spec_p(TM_ROWWISE), TM_ROWWISE)
        aux = _aux_proj(h, w_in[l], tm_p)
        qkv16, k32, v32 = _qkv_proj(h, w_in[l], cos_p, sin_p, tm_p, BF16)
        pool_out = _pool_prompt(aux, w_pool_grp[l], pool_scale[l], seq)
        attn_out = _attn_prompt(lam_vecs, qkv16, aux, g_subln[l], batch, seq, lam_init)
        merged = _merge(pool_out, attn_out, aux, wpp, wpa, tm_p)
        xp_new = _out_proj(merged, wout, xp, mod_p, spec_p(TM_ROWWISE), g_post[l], TM_ROWWISE)
        outs["kp"].append(k32.reshape(batch, seq, N_HEADS, 2, HEAD_DIM))
        outs["vp"].append(v32.reshape(batch, seq, N_HEADS, V_DIM))
        outs["up"].append(aux.reshape(batch, seq, AUX_WIDTH)[:, seq - POOL_BUF:, :POOL_WIDTH])

        ms = n_new * dec_b
        h_s = _modulated_norm(xs, g_pre[l], mod_s, spec_s, dec_b)
        aux_s = _aux_proj(h_s, w_in[l], ms)
        qkv_s, k32_s, v32_s = _qkv_proj(h_s, w_in[l], cos_s, sin_s, ms, F32)
        sp_t = state_pool[l].transpose(1, 0, 2)
        pool_out_s = _pool_sample(sp_t, aux_s.reshape(n_new, dec_b, AUX_WIDTH), w_pool_grp[l], pool_scale[l])
        attn_out_s = _attn_decode(
            page_table, lam_vecs,
            qkv_s.reshape(n_new, dec_b, 3, N_HEADS, V_DIM),
            aux_s.reshape(n_new, dec_b, AUX_WIDTH // V_DIM, V_DIM),
            g_subln[l], ck, cv, l * n_pool_pages, lam_init)
        merged_s = _merge(pool_out_s, attn_out_s.reshape(ms, width), aux_s, wpp, wpa, ms)
        xs_new = _out_proj(merged_s, wout, xs, mod_s, spec_s, g_post[l], ms)

        def to_bt(a):
            return a.reshape(n_new, dec_b, -1).transpose(1, 0, 2)

        outs["ks"].append(to_bt(k32_s).reshape(dec_b, n_new, N_HEADS, 2, HEAD_DIM))
        outs["vs"].append(to_bt(v32_s).reshape(dec_b, n_new, N_HEADS, V_DIM))
        u_ext = jnp.concatenate([state_pool[l], to_bt(aux_s[:, :POOL_WIDTH])], axis=1)
        outs["us"].append(u_ext[:, -POOL_BUF:])
        xp, xs = xp_new, xs_new

    yp = xp.reshape(batch, seq, d)
    ys = xs.reshape(n_new, dec_b, d).transpose(1, 0, 2)
    return (yp, ys, jnp.stack(outs["kp"]), jnp.stack(outs["vp"]), jnp.stack(outs["up"]),
            jnp.stack(outs["ks"]), jnp.stack(outs["vs"]), jnp.stack(outs["us"]))
```

```python
import functools
import math

import jax
import jax.numpy as jnp
from jax import lax
from jax.experimental import pallas as pl
from jax.experimental.pallas import tpu as pltpu

F32 = jnp.float32
BF16 = jnp.bfloat16

EPS = 1e-6
ROPE_THETA = 10000.0
N_HEADS = 8
HEAD_DIM = 128
V_DIM = 2 * HEAD_DIM
D_MODEL = 2048
POOL_WIDTH = 1024
POOL_GROUP = 256
POOL_WINDOWS = (2, 4, 8, 16)
POOL_BUF = 15
LANES = 128
COL_CHUNK = 256
TM_PROJ = 1024
TM_ROWWISE = 512
TN_ADALN = 2048
PAGE_LOOKAHEAD = 4
PAGE_SLOTS = PAGE_LOOKAHEAD + 1
HALO = 16
AUX_WIDTH = 8192
VMEM_LIMIT = 56 * 1024 * 1024


def _params(sem):
    return pltpu.CompilerParams(dimension_semantics=sem, vmem_limit_bytes=VMEM_LIMIT)


def _dot(a, b):
    return jnp.dot(a, b, preferred_element_type=F32)


def _dot_nt(a, b):
    return lax.dot_general(a, b, (((1,), (1,)), ((), ())), preferred_element_type=F32)


def _sigmoid(x):
    return jax.nn.sigmoid(x)


def _mod_kernel(c_ref, w_ref, b_ref, o_ref):
    c = c_ref[...]
    a = (c * _sigmoid(c)).astype(BF16)
    o_ref[...] = _dot(a, w_ref[...].astype(BF16)) + b_ref[...]


def _mod(c_all, w_ada, b_ada, tn=512):
    m, d = c_all.shape
    n = w_ada.shape[1]
    return pl.pallas_call(
        _mod_kernel,
        grid=(n // tn,),
        in_specs=[pl.BlockSpec((m, d), lambda j: (0, 0)),
                  pl.BlockSpec((d, tn), lambda j: (0, j)),
                  pl.BlockSpec((1, tn), lambda j: (0, j))],
        out_specs=pl.BlockSpec((m, tn), lambda j: (0, j)),
        out_shape=jax.ShapeDtypeStruct((m, n), F32),
        compiler_params=_params(("parallel",)),
        name="adaln_mod",
    )(c_all, w_ada, b_ada.reshape(1, n))


def _h_kernel(x_ref, g_ref, sc_ref, sh_ref, o_ref):
    x = x_ref[...]
    ms = jnp.mean(x * x, axis=-1, keepdims=True)
    y = x * lax.rsqrt(ms + EPS) * g_ref[...]
    o_ref[...] = (y * (1.0 + sc_ref[...]) + sh_ref[...]).astype(BF16)


def _modulated_norm(x2d, g_pre, mod, mod_spec, tm):
    m, d = x2d.shape
    return pl.pallas_call(
        _h_kernel,
        grid=(m // tm,),
        in_specs=[pl.BlockSpec((tm, d), lambda i: (i, 0)),
                  pl.BlockSpec((1, d), lambda i: (0, 0)),
                  mod_spec(1), mod_spec(0)],
        out_specs=pl.BlockSpec((tm, d), lambda i: (i, 0)),
        out_shape=jax.ShapeDtypeStruct((m, d), BF16),
        compiler_params=_params(("parallel",)),
        name="modulated_norm",
    )(x2d, g_pre.reshape(1, d), mod, mod)


def _col_chunks(tn):
    return [(c * COL_CHUNK, COL_CHUNK) for c in range(tn // COL_CHUNK)]


def _aux_kernel(h_ref, w_ref, o_ref, *, tn):
    j = pl.program_id(1)
    n_plain = POOL_WIDTH // tn
    n_silu_end = (2 * POOL_WIDTH + D_MODEL) // tn

    def run(epilogue):
        for c0, cw in _col_chunks(tn):
            acc = _dot(h_ref[...], w_ref[:, c0:c0 + cw].astype(BF16))
            o_ref[:, c0:c0 + cw] = epilogue(acc)

    @pl.when(j < n_plain)
    def _():
        run(lambda a: a)

    @pl.when((j >= n_plain) & (j < n_silu_end))
    def _():
        run(lambda a: a * _sigmoid(a))

    @pl.when(j >= n_silu_end)
    def _():
        run(_sigmoid)


def _aux_proj(h, w_in, tm, tn=1024):
    m, d = h.shape
    n_head = (2 * POOL_WIDTH) // tn
    skip = (3 * D_MODEL) // tn
    return pl.pallas_call(
        functools.partial(_aux_kernel, tn=tn),
        grid=(m // tm, AUX_WIDTH // tn),
        in_specs=[pl.BlockSpec((tm, d), lambda i, j: (i, 0)),
                  pl.BlockSpec((d, tn), lambda i, j: (0, jnp.where(j < n_head, j, j + skip)))],
        out_specs=pl.BlockSpec((tm, tn), lambda i, j: (i, j)),
        out_shape=jax.ShapeDtypeStruct((m, AUX_WIDTH), F32),
        compiler_params=_params(("parallel", "arbitrary")),
        name="aux_proj",
    )(h, w_in)


def _rope(acc, cos, sin_signed):
    parts = []
    for c in range(acc.shape[1] // LANES):
        xs = acc[:, c * LANES:(c + 1) * LANES]
        parts.append(xs * cos + pltpu.roll(xs, HEAD_DIM // 2, axis=1) * sin_signed)
    return jnp.concatenate(parts, axis=1)


def _qkv_kernel(h_ref, w_ref, cos_ref, sin_ref, lo_ref, k_ref, v_ref, *, tm, tn):
    j = pl.program_id(1)
    nb = D_MODEL // tn
    n_hm = 2 * N_HEADS

    def chunk(c0, cw):
        return _dot(h_ref[...], w_ref[:, c0:c0 + cw].astype(BF16))

    @pl.when(j < nb)
    def _():
        for c0, cw in _col_chunks(tn):
            r = _rope(chunk(c0, cw), cos_ref[...], sin_ref[...])
            lo_ref[:, c0:c0 + cw] = r.astype(BF16).astype(lo_ref.dtype)

    @pl.when((j >= nb) & (j < 2 * nb))
    def _():
        for c0, cw in _col_chunks(tn):
            r = _rope(chunk(c0, cw), cos_ref[...], sin_ref[...])
            lo_ref[:, c0:c0 + cw] = r.astype(BF16).astype(lo_ref.dtype)
            for e in range(cw // LANES):
                hm = (j - nb) * (tn // LANES) + c0 // LANES + e
                k_ref[pl.ds(hm, tm, stride=n_hm), :] = r[:, e * LANES:(e + 1) * LANES]

    @pl.when(j >= 2 * nb)
    def _():
        for c0, cw in _col_chunks(tn):
            acc = chunk(c0, cw)
            v_ref[:, c0:c0 + cw] = acc
            lo_ref[:, c0:c0 + cw] = acc.astype(BF16).astype(lo_ref.dtype)


def _qkv_proj(h, w_in, cos2, sin2, tm, lo_dtype, tn=512):
    m, d = h.shape
    nb = D_MODEL // tn
    n_hm = 2 * N_HEADS
    col0 = (2 * POOL_WIDTH) // tn
    n_pos = cos2.shape[0] // tm
    return pl.pallas_call(
        functools.partial(_qkv_kernel, tm=tm, tn=tn),
        grid=(m // tm, 3 * nb),
        in_specs=[pl.BlockSpec((tm, d), lambda i, j: (i, 0)),
                  pl.BlockSpec((d, tn), lambda i, j: (0, col0 + j)),
                  pl.BlockSpec((tm, LANES), lambda i, j: (i % n_pos, 0)),
                  pl.BlockSpec((tm, LANES), lambda i, j: (i % n_pos, 0))],
        out_specs=[pl.BlockSpec((tm, tn), lambda i, j: (i, j)),
                   pl.BlockSpec((tm * n_hm, HEAD_DIM), lambda i, j: (i, 0)),
                   pl.BlockSpec((tm, tn), lambda i, j: (i, jnp.clip(j - 2 * nb, 0, nb - 1)))],
        out_shape=[jax.ShapeDtypeStruct((m, 3 * D_MODEL), lo_dtype),
                   jax.ShapeDtypeStruct((m * n_hm, HEAD_DIM), F32),
                   jax.ShapeDtypeStruct((m, D_MODEL), F32)],
        compiler_params=_params(("parallel", "arbitrary")),
        name="qkv_proj",
    )(h, w_in, cos2, sin2)


def _pool_mix(pooled_groups, wg_ref, scale_ref, szp):
    mixed = [_dot(p.astype(BF16), wg_ref[g].astype(BF16)) for g, p in enumerate(pooled_groups)]
    mixed = jnp.concatenate(mixed, axis=1)
    return (mixed * scale_ref[...] * szp).astype(BF16)


def _pool_prompt_kernel(u_ref, halo_ref, szp_ref, wg_ref, scale_ref, o_ref, *, tp, tiles_per_seq):
    i = pl.program_id(0)
    first = (i % tiles_per_seq) == 0
    halo = jnp.where(first, 0.0, halo_ref[...])
    ext = jnp.concatenate([halo, u_ref[...]], axis=0)
    pos = (i % tiles_per_seq) * tp + lax.broadcasted_iota(jnp.int32, (tp, 1), 0)
    groups = []
    for g, w in enumerate(POOL_WINDOWS):
        e = ext[:, g * POOL_GROUP:(g + 1) * POOL_GROUP]
        a = e
        s = 1
        while s < w:
            a = a + pltpu.roll(a, s, axis=0)
            s *= 2
        cnt = jnp.minimum(pos + 1, w).astype(F32)
        groups.append(a[HALO:] / cnt - e[HALO:])
    o_ref[...] = _pool_mix(groups, wg_ref, scale_ref, szp_ref[...])


def _pool_prompt(aux, w_pool_grp, pool_scale, seq, tp=512):
    m = aux.shape[0]
    tiles_per_seq = seq // tp
    hb = tp // HALO
    return pl.pallas_call(
        functools.partial(_pool_prompt_kernel, tp=tp, tiles_per_seq=tiles_per_seq),
        grid=(m // tp,),
        in_specs=[pl.BlockSpec((tp, POOL_WIDTH), lambda i: (i, 0)),
                  pl.BlockSpec((HALO, POOL_WIDTH), lambda i: (jnp.maximum(i * hb - 1, 0), 0)),
                  pl.BlockSpec((tp, POOL_WIDTH), lambda i: (i, 1)),
                  pl.BlockSpec((4, POOL_GROUP, POOL_GROUP), lambda i: (0, 0, 0)),
                  pl.BlockSpec((1, POOL_WIDTH), lambda i: (0, 0))],
        out_specs=pl.BlockSpec((tp, POOL_WIDTH), lambda i: (i, 0)),
        out_shape=jax.ShapeDtypeStruct((m, POOL_WIDTH), BF16),
        compiler_params=_params(("parallel",)),
        name="pool_prompt",
    )(aux, aux, aux, w_pool_grp, pool_scale.reshape(1, POOL_WIDTH))


def _pool_sample_kernel(sp_ref, u_ref, szp_ref, wg_ref, scale_ref, o_ref, *, n_new):
    nb = u_ref.shape[1]
    ext = [sp_ref[r] for r in range(POOL_BUF)] + [u_ref[t] for t in range(n_new)]
    groups = []
    for g, w in enumerate(POOL_WINDOWS):
        per_t = []
        for t in range(n_new):
            end = POOL_BUF + t
            acc = ext[end][:, g * POOL_GROUP:(g + 1) * POOL_GROUP]
            cur = acc
            for r in range(end - w + 1, end):
                acc = acc + ext[r][:, g * POOL_GROUP:(g + 1) * POOL_GROUP]
            per_t.append(acc / float(w) - cur)
        groups.append(jnp.concatenate(per_t, axis=0))
    szp = szp_ref[...].reshape(n_new * nb, POOL_WIDTH)
    o_ref[...] = _pool_mix(groups, wg_ref, scale_ref, szp)


def _pool_sample(sp_t, aux_s3, w_pool_grp, pool_scale):
    n_new, nb, _ = aux_s3.shape
    return pl.pallas_call(
        functools.partial(_pool_sample_kernel, n_new=n_new),
        grid=(1,),
        in_specs=[pl.BlockSpec((POOL_BUF, nb, POOL_WIDTH), lambda i: (0, 0, 0)),
                  pl.BlockSpec((n_new, nb, POOL_WIDTH), lambda i: (0, 0, 0)),
                  pl.BlockSpec((n_new, nb, POOL_WIDTH), lambda i: (0, 0, 1)),
                  pl.BlockSpec((4, POOL_GROUP, POOL_GROUP), lambda i: (0, 0, 0)),
                  pl.BlockSpec((1, POOL_WIDTH), lambda i: (0, 0))],
        out_specs=pl.BlockSpec((n_new * nb, POOL_WIDTH), lambda i: (0, 0)),
        out_shape=jax.ShapeDtypeStruct((n_new * nb, POOL_WIDTH), BF16),
        compiler_params=_params(("arbitrary",)),
        name="pool_sample",
    )(sp_t, aux_s3, aux_s3, w_pool_grp, pool_scale.reshape(1, POOL_WIDTH))


def _lambda(lv, lam_init):
    l1 = jnp.sum(lv[0:1] * lv[1:2], axis=1, keepdims=True)
    l2 = jnp.sum(lv[2:3] * lv[3:4], axis=1, keepdims=True)
    return jnp.exp(l1) - jnp.exp(l2) + lam_init


def _subln_gate(o, g, sz, lam_init):
    ms = jnp.mean(o * o, axis=-1, keepdims=True)
    y = o * lax.rsqrt(ms + EPS) * g
    return (y * (1.0 - lam_init) * sz).astype(BF16)


def _attn_prompt_kernel(lv_ref, q_ref, k_ref, v_ref, sz_ref, g_ref, o_ref, s_ref,
                        *, tq, nq, lam_init, scale):
    qi = pl.program_id(2)
    lam = _lambda(lv_ref[...], lam_init)
    c_exp = scale * math.log2(math.e)

    half = tq // 2

    def fold(x, op):
        r = x[:, :LANES]
        for c in range(1, x.shape[1] // LANES):
            r = op(r, x[:, c * LANES:(c + 1) * LANES])
        return r

    def tile(n):
        row = lax.broadcasted_iota(jnp.int32, (half, half), 0)
        col = lax.broadcasted_iota(jnp.int32, (half, half), 1)
        causal = col <= row
        causal_full = (lax.broadcasted_iota(jnp.int32, (tq, tq), 1)
                       <= lax.broadcasted_iota(jnp.int32, (tq, tq), 0))
        k0 = n * tq
        pv, rinv = [], []
        q = [q_ref[:, mp * HEAD_DIM:(mp + 1) * HEAD_DIM] for mp in range(2)]
        m_run = [None, None]
        for kc in range(n + 1):
            for mp in range(2):
                kblk = k_ref[kc * tq:(kc + 1) * tq, mp * HEAD_DIM:(mp + 1) * HEAD_DIM]
                if kc < n:
                    s = _dot_nt(q[mp], kblk) * c_exp
                    s_ref[mp, :, kc * tq:(kc + 1) * tq] = s
                    f = fold(s, jnp.maximum)
                elif n == 0:
                    s = jnp.where(causal_full, _dot_nt(q[mp], kblk) * c_exp, -jnp.inf)
                    s_ref[mp, :, 0:tq] = s
                    f = fold(s, jnp.maximum)
                else:
                    s_top = jnp.where(causal, _dot_nt(q[mp][:half], kblk[:half]) * c_exp, -jnp.inf)
                    s_bot = _dot_nt(q[mp][half:], kblk) * c_exp
                    s_bot = jnp.concatenate(
                        [s_bot[:, :half], jnp.where(causal, s_bot[:, half:], -jnp.inf)], axis=1)
                    s_ref[mp, :half, k0:k0 + half] = s_top
                    s_ref[mp, half:, k0:k0 + tq] = s_bot
                    f = jnp.concatenate([fold(s_top, jnp.maximum), fold(s_bot, jnp.maximum)], axis=0)
                m_run[mp] = f if m_run[mp] is None else jnp.maximum(m_run[mp], f)
        for mp in range(2):
            mx = jnp.max(m_run[mp], axis=1, keepdims=True)
            l_run = None
            acc = None
            for kc in range(n + 1):
                vblk = v_ref[kc * tq:(kc + 1) * tq, :]
                if kc < n or n == 0:
                    p = jnp.exp2(s_ref[mp, :, kc * tq:(kc + 1) * tq] - mx)
                    f = fold(p, jnp.add)
                    d = _dot(p.astype(BF16), vblk)
                else:
                    p_top = jnp.exp2(s_ref[mp, :half, k0:k0 + half] - mx[:half])
                    p_bot = jnp.exp2(s_ref[mp, half:, k0:k0 + tq] - mx[half:])
                    f = jnp.concatenate([fold(p_top, jnp.add), fold(p_bot, jnp.add)], axis=0)
                    d = jnp.concatenate([_dot(p_top.astype(BF16), vblk[:half]),
                                         _dot(p_bot.astype(BF16), vblk)], axis=0)
                l_run = f if l_run is None else l_run + f
                acc = d if acc is None else acc + d
            rinv.append(1.0 / jnp.sum(l_run, axis=1, keepdims=True))
            pv.append(acc)
        o = pv[0] * rinv[0] - pv[1] * (lam * rinv[1])
        o_ref[...] = _subln_gate(o, g_ref[...], sz_ref[...], lam_init)

    for n in range(nq):
        pl.when(qi == n)(functools.partial(tile, n))


def _attn_prompt(lam_vecs, qkv16, aux, g_subln, batch, seq, lam_init, tq=512):
    m = qkv16.shape[0]
    nq = seq // tq
    sza0 = (2 * POOL_WIDTH) // V_DIM
    kern = functools.partial(_attn_prompt_kernel, tq=tq, nq=nq, lam_init=lam_init, scale=HEAD_DIM ** -0.5)
    return pl.pallas_call(
        kern,
        grid=(batch, N_HEADS, nq),
        in_specs=[pl.BlockSpec((4, HEAD_DIM), lambda b, h, qi: (0, 0)),
                  pl.BlockSpec((tq, V_DIM), lambda b, h, qi: (b * nq + qi, h)),
                  pl.BlockSpec((seq, V_DIM), lambda b, h, qi: (b, N_HEADS + h)),
                  pl.BlockSpec((seq, V_DIM), lambda b, h, qi: (b, 2 * N_HEADS + h)),
                  pl.BlockSpec((tq, V_DIM), lambda b, h, qi: (b * nq + qi, sza0 + h)),
                  pl.BlockSpec((1, V_DIM), lambda b, h, qi: (0, 0))],
        out_specs=pl.BlockSpec((tq, V_DIM), lambda b, h, qi: (b * nq + qi, h)),
        out_shape=jax.ShapeDtypeStruct((m, N_HEADS * V_DIM), BF16),
        scratch_shapes=[pltpu.VMEM((2, tq, seq), F32)],
        compiler_params=_params(("parallel", "parallel", "arbitrary")),
        name="attn_prompt",
    )(lam_vecs, qkv16, qkv16, qkv16, aux, g_subln.reshape(1, V_DIM))


def _attn_decode_kernel(pt_ref, lv_ref, q_ref, kn_ref, vn_ref, sz_ref, g_ref, ck_hbm, cv_hbm, o_ref,
                        kbuf, vbuf, ksem, vsem, s_ref, p_ref, pn_ref, acc_ref,
                        *, pps, nchunk, n_new, page_base, lam_init, scale):
    b = pl.program_id(0)
    j = pl.program_id(1)
    nstep = 2 * nchunk
    g = b * nstep + j
    total = pl.num_programs(0) * nstep
    nrow = n_new * N_HEADS
    ncol = vbuf.shape[2]

    def page_copies(gg, apply):
        gg = jnp.asarray(gg, jnp.int32)
        bb = gg // nstep
        jj = gg % nstep

        @pl.when(jj < nchunk)
        def _():
            slot = (bb * nchunk + jj) % PAGE_SLOTS
            for i in range(pps):
                page = page_base + pt_ref[bb, jj * pps + i]
                apply(pltpu.make_async_copy(ck_hbm.at[page], kbuf.at[slot, i], ksem.at[slot, i]))

        @pl.when(jj >= nchunk)
        def _():
            c = jj - nchunk
            slot = (bb * nchunk + c) % PAGE_SLOTS
            for i in range(pps):
                page = page_base + pt_ref[bb, c * pps + i]
                apply(pltpu.make_async_copy(cv_hbm.at[page], vbuf.at[slot, i], vsem.at[slot, i]))

    @pl.when(g == 0)
    def _():
        for d in range(PAGE_LOOKAHEAD):
            page_copies(d, lambda cp: cp.start())

    @pl.when(g + PAGE_LOOKAHEAD < total)
    def _():
        page_copies(g + PAGE_LOOKAHEAD, lambda cp: cp.start())

    page_copies(g, lambda cp: cp.wait())
    slot = (b * nchunk + jnp.where(j < nchunk, j, j - nchunk)) % PAGE_SLOTS
    k_refs = [kbuf.at[slot, i] for i in range(pps)]
    v_refs = [vbuf.at[slot, i] for i in range(pps)]

    @pl.when(j < nchunk)
    def _():
        qq = q_ref[...].reshape(nrow, V_DIM)
        rows = lax.broadcasted_iota(jnp.int32, (nrow, ncol), 0)
        cols = lax.broadcasted_iota(jnp.int32, (nrow, ncol), 1)
        valid = (cols & (N_HEADS - 1)) == (rows & (N_HEADS - 1))
        for mp in range(2):
            qm = qq[:, mp * HEAD_DIM:(mp + 1) * HEAD_DIM].astype(BF16)
            for i in range(pps):
                km = k_refs[i][pl.ds(mp, ncol, stride=2), :].astype(BF16)
                s = _dot_nt(qm, km) * scale
                s_ref[j, mp, :, i * ncol:(i + 1) * ncol] = jnp.where(valid, s, -jnp.inf)

    @pl.when(j == nchunk - 1)
    def _():
        lam = _lambda(lv_ref[...], lam_init)
        qq = q_ref[...].reshape(nrow, V_DIM)
        lane = lax.broadcasted_iota(jnp.int32, (nrow, LANES), 1)
        qrow = lax.broadcasted_iota(jnp.int32, (nrow, LANES), 0) >> 3
        sn = [jnp.full((nrow, LANES), -jnp.inf, F32) for _ in range(2)]
        for t in range(n_new):
            ktile = jnp.concatenate([kn_ref[t]] * n_new, axis=0)
            prod = qq * ktile
            for mp in range(2):
                col = jnp.sum(prod[:, mp * HEAD_DIM:(mp + 1) * HEAD_DIM], axis=1, keepdims=True) * scale
                sn[mp] = jnp.where((lane == t) & (qrow >= t), col, sn[mp])
        pn = []
        rinv = []
        for mp in range(2):
            mx = jnp.max(sn[mp], axis=1, keepdims=True)
            for c in range(nchunk):
                mx = jnp.maximum(mx, jnp.max(s_ref[c, mp], axis=1, keepdims=True))
            pnew = jnp.exp(sn[mp] - mx)
            l = jnp.sum(pnew, axis=1, keepdims=True)
            for c in range(nchunk):
                p = jnp.exp(s_ref[c, mp] - mx)
                s_ref[c, mp] = p
                l = l + jnp.sum(p, axis=1, keepdims=True)
            pn.append(pnew)
            rinv.append(1.0 / l)
        r1 = rinv[0]
        r2 = lam * rinv[1]
        for c in range(nchunk):
            p_ref[c] = (s_ref[c, 0] * r1 - s_ref[c, 1] * r2).astype(BF16)
        pn_ref[...] = (pn[0] * r1 - pn[1] * r2).astype(BF16).astype(F32)

    @pl.when(j == nchunk)
    def _():
        acc_ref[...] = jnp.zeros_like(acc_ref)

    @pl.when(j >= nchunk)
    def _():
        c = j - nchunk
        a = None
        for i in range(pps):
            vp = v_refs[i][...].astype(BF16)
            d = _dot(p_ref[c, :, i * ncol:(i + 1) * ncol], vp)
            a = d if a is None else a + d
        acc_ref[...] += a

    @pl.when(j == 2 * nchunk - 1)
    def _():
        o = acc_ref[...]
        pnew = pn_ref[...]
        for t in range(n_new):
            vtile = jnp.concatenate([vn_ref[t]] * n_new, axis=0)
            o = o + pnew[:, t:t + 1] * vtile
        y = _subln_gate(o, g_ref[...], sz_ref[...].reshape(nrow, V_DIM), lam_init)
        o_ref[...] = y.astype(o_ref.dtype).reshape(n_new, N_HEADS, V_DIM)


def _attn_decode(page_table, lam_vecs, qkv_s5, aux_s4, g_subln, ck, cv, page_base, lam_init, pps=4):
    n_new, dec_b = qkv_s5.shape[:2]
    n_pages = page_table.shape[1]
    nchunk = n_pages // pps
    nrow = n_new * N_HEADS
    ncol = cv.shape[1]
    sza0 = (2 * POOL_WIDTH) // (N_HEADS * V_DIM)

    def small(sel):
        return pl.BlockSpec((n_new, None, None, N_HEADS, V_DIM), lambda b, j, pt: (0, b, sel, 0, 0))

    kern = functools.partial(_attn_decode_kernel, pps=pps, nchunk=nchunk, n_new=n_new, page_base=page_base,
                             lam_init=lam_init, scale=HEAD_DIM ** -0.5)
    grid_spec = pltpu.PrefetchScalarGridSpec(
        num_scalar_prefetch=1,
        grid=(dec_b, 2 * nchunk),
        in_specs=[pl.BlockSpec((4, HEAD_DIM), lambda b, j, pt: (0, 0)),
                  small(0), small(1), small(2),
                  pl.BlockSpec((n_new, None, N_HEADS, V_DIM), lambda b, j, pt: (0, b, sza0, 0)),
                  pl.BlockSpec((1, V_DIM), lambda b, j, pt: (0, 0)),
                  pl.BlockSpec(memory_space=pl.ANY),
                  pl.BlockSpec(memory_space=pl.ANY)],
        out_specs=pl.BlockSpec((n_new, None, N_HEADS, V_DIM), lambda b, j, pt: (0, b, 0, 0)),
        scratch_shapes=[pltpu.VMEM((PAGE_SLOTS, pps, 2 * ncol, HEAD_DIM), F32),
                        pltpu.VMEM((PAGE_SLOTS, pps, ncol, V_DIM), F32),
                        pltpu.SemaphoreType.DMA((PAGE_SLOTS, pps)),
                        pltpu.SemaphoreType.DMA((PAGE_SLOTS, pps)),
                        pltpu.VMEM((nchunk, 2, nrow, pps * ncol), F32),
                        pltpu.VMEM((nchunk, nrow, pps * ncol), BF16),
                        pltpu.VMEM((nrow, LANES), F32),
                        pltpu.VMEM((nrow, V_DIM), F32)],
    )
    return pl.pallas_call(
        kern,
        grid_spec=grid_spec,
        out_shape=jax.ShapeDtypeStruct((n_new, dec_b, N_HEADS, V_DIM), F32),
        compiler_params=_params(("arbitrary", "arbitrary")),
        name="attn_decode",
    )(page_table, lam_vecs, qkv_s5, qkv_s5, qkv_s5, aux_s4, g_subln.reshape(1, V_DIM), ck, cv)


def _merge_kernel(po_ref, ao_ref, wpp_ref, wpa_ref, gp_ref, ga_ref, o_ref):
    pp = _dot(po_ref[...].astype(BF16), wpp_ref[...].astype(BF16))
    pa = _dot(ao_ref[...].astype(BF16), wpa_ref[...].astype(BF16))
    o_ref[...] = (gp_ref[...] * pp + ga_ref[...] * pa).astype(BF16)


def _merge(pool_out, attn_out, aux, w_proj_pool, w_proj_attn, tm, tn=512):
    m = pool_out.shape[0]
    gp0 = (2 * POOL_WIDTH + D_MODEL) // tn
    ga0 = gp0 + D_MODEL // tn
    return pl.pallas_call(
        _merge_kernel,
        grid=(m // tm, D_MODEL // tn),
        in_specs=[pl.BlockSpec((tm, POOL_WIDTH), lambda i, j: (i, 0)),
                  pl.BlockSpec((tm, D_MODEL), lambda i, j: (i, 0)),
                  pl.BlockSpec((POOL_WIDTH, tn), lambda i, j: (0, j)),
                  pl.BlockSpec((D_MODEL, tn), lambda i, j: (0, j)),
                  pl.BlockSpec((tm, tn), lambda i, j: (i, gp0 + j)),
                  pl.BlockSpec((tm, tn), lambda i, j: (i, ga0 + j))],
        out_specs=pl.BlockSpec((tm, tn), lambda i, j: (i, j)),
        out_shape=jax.ShapeDtypeStruct((m, D_MODEL), BF16),
        compiler_params=_params(("parallel", "arbitrary")),
        name="gated_merge",
    )(pool_out, attn_out, w_proj_pool, w_proj_attn, aux, aux)


def _out_kernel(m_ref, w_ref, x_ref, gate_ref, g_ref, y_ref, acc_ref, *, tn, nj):
    j = pl.program_id(1)
    acc_ref[j] = _dot(m_ref[...], w_ref[...].astype(BF16))

    @pl.when(j == nj - 1)
    def _():
        ss = None
        for c in range(nj):
            a = acc_ref[c]
            t = jnp.sum(a * a, axis=-1, keepdims=True)
            ss = t if ss is None else ss + t
        r = lax.rsqrt(ss / float(nj * tn) + EPS)
        tm, gate_rows = x_ref.shape[0], gate_ref.shape[0]
        for c in range(nj):
            sl = slice(c * tn, (c + 1) * tn)
            gate = gate_ref[:, sl]
            if gate_rows not in (1, tm):
                gate = jnp.concatenate([gate] * (tm // gate_rows), axis=0)
            y_ref[:, sl] = x_ref[:, sl] + gate * (acc_ref[c] * r * g_ref[:, sl])


def _out_proj(merged, w_out, x2d, mod, mod_spec2, g_post, tm, tn=512):
    m, d = x2d.shape
    nj = d // tn
    return pl.pallas_call(
        functools.partial(_out_kernel, tn=tn, nj=nj),
        grid=(m // tm, nj),
        in_specs=[pl.BlockSpec((tm, d), lambda i, j: (i, 0)),
                  pl.BlockSpec((d, tn), lambda i, j: (0, j)),
                  pl.BlockSpec((tm, d), lambda i, j: (i, 0)),
                  mod_spec2(2),
                  pl.BlockSpec((1, d), lambda i, j: (0, 0))],
        out_specs=pl.BlockSpec((tm, d), lambda i, j: (i, 0)),
        out_shape=jax.ShapeDtypeStruct((m, d), F32),
        scratch_shapes=[pltpu.VMEM((nj, tm, tn), F32)],
        compiler_params=_params(("parallel", "arbitrary")),
        name="out_proj",
    )(merged, w_out, x2d, mod, g_post.reshape(1, d))


def _rope_tables(pos):
    half = HEAD_DIM // 2
    inv_freq = ROPE_THETA ** (-jnp.arange(half, dtype=F32) / half)
    ang = pos.astype(F32)[:, None] * inv_freq[None, :]
    cos = jnp.cos(ang)
    sin = jnp.sin(ang)
    return jnp.concatenate([cos, cos], axis=1), jnp.concatenate([-sin, sin], axis=1)


def kernel(x_prompt, x_sample, cache_k, cache_v, state_pool, page_table, c_prompt, c_sample,
           w_ada, b_ada, g_pre, g_post, w_in, w_pool_grp, pool_scale,
           lambda_q1, lambda_k1, lambda_q2, lambda_k2, g_subln, w_proj_pool, w_proj_attn, w_out):
    batch, seq, d = x_prompt.shape
    dec_b, n_new, _ = x_sample.shape
    depth = w_in.shape[0]
    n_pages = page_table.shape[1]
    n_pool_pages, page_size = cache_k.shape[1], cache_k.shape[2]
    past_len = n_pages * page_size
    width = N_HEADS * V_DIM
    ck = cache_k.reshape(depth * n_pool_pages, page_size * N_HEADS * 2, HEAD_DIM)
    cv = cache_v.reshape(depth * n_pool_pages, page_size * N_HEADS, V_DIM)

    cos_p, sin_p = _rope_tables(jnp.arange(seq))
    cos_s, sin_s = _rope_tables(past_len + jnp.repeat(jnp.arange(n_new), dec_b))

    xp = x_prompt.reshape(batch * seq, d)
    xs = x_sample.transpose(1, 0, 2).reshape(n_new * dec_b, d)
    c_all = jnp.concatenate([c_prompt, c_sample], axis=0)
    tm_p = TM_PROJ

    outs = {name: [] for name in ("kp", "vp", "up", "ks", "vs", "us")}
    for l in range(depth):
        lam_init = 0.8 - 0.6 * math.exp(-0.3 * l)
        lam_vecs = jnp.stack([lambda_q1[l], lambda_k1[l], lambda_q2[l], lambda_k2[l]])
        mod = _mod(c_all, w_ada[l], b_ada[l], TN_ADALN)
        wpp, wpa, wout = (w.astype(BF16) for w in (w_proj_pool[l], w_proj_attn[l], w_out[l]))
        mod_p = mod[:batch].reshape(batch, 1, 3 * d)
        mod_s = mod[batch:]

        def spec_p(tm):
            per_seq = seq // tm
            return lambda col: pl.BlockSpec((None, 1, d), lambda i, *_: (i // per_seq, 0, col))

        def spec_s(col):
            return pl.BlockSpec((dec_b, d), lambda i, *_: (0, col))

        h = _modulated_norm(xp, g_pre[l], mod_p, spec_p(TM_ROWWISE), TM_ROWWISE)
        aux = _aux_proj(h, w_in[l], tm_p)
        qkv16, k32, v32 = _qkv_proj(h, w_in[l], cos_p, sin_p, tm_p, BF16)
        pool_out = _pool_prompt(aux, w_pool_grp[l], pool_scale[l], seq)
        attn_out = _attn_prompt(lam_vecs, qkv16, aux, g_subln[l], batch, seq, lam_init)
        merged = _merge(pool_out, attn_out, aux, wpp, wpa, tm_p)
        xp_new = _out_proj(merged, wout, xp, mod_p, spec_p(TM_ROWWISE), g_post[l], TM_ROWWISE)
        outs["kp"].append(k32.reshape(batch, seq, N_HEADS, 2, HEAD_DIM))
        outs["vp"].append(v32.reshape(batch, seq, N_HEADS, V_DIM))
        outs["up"].append(aux.reshape(batch, seq, AUX_WIDTH)[:, seq - POOL_BUF:, :POOL_WIDTH])

        ms = n_new * dec_b
        h_s = _modulated_norm(xs, g_pre[l], mod_s, spec_s, dec_b)
        aux_s = _aux_proj(h_s, w_in[l], ms)
        qkv_s, k32_s, v32_s = _qkv_proj(h_s, w_in[l], cos_s, sin_s, ms, F32)
        sp_t = state_pool[l].transpose(1, 0, 2)
        pool_out_s = _pool_sample(sp_t, aux_s.reshape(n_new, dec_b, AUX_WIDTH), w_pool_grp[l], pool_scale[l])
        attn_out_s = _attn_decode(
            page_table, lam_vecs,
            qkv_s.reshape(n_new, dec_b, 3, N_HEADS, V_DIM),
            aux_s.reshape(n_new, dec_b, AUX_WIDTH // V_DIM, V_DIM),
            g_subln[l], ck, cv, l * n_pool_pages, lam_init)
        merged_s = _merge(pool_out_s, attn_out_s.reshape(ms, width), aux_s, wpp, wpa, ms)
        xs_new = _out_proj(merged_s, wout, xs, mod_s, spec_s, g_post[l], ms)

        def to_bt(a):
            return a.reshape(n_new, dec_b, -1).transpose(1, 0, 2)

        outs["ks"].append(to_bt(k32_s).reshape(dec_b, n_new, N_HEADS, 2, HEAD_DIM))
        outs["vs"].append(to_bt(v32_s).reshape(dec_b, n_new, N_HEADS, V_DIM))
        u_ext = jnp.concatenate([state_pool[l], to_bt(aux_s[:, :POOL_WIDTH])], axis=1)
        outs["us"].append(u_ext[:, -POOL_BUF:])
        xp, xs = xp_new, xs_new

    yp = xp.reshape(batch, seq, d)
    ys = xs.reshape(n_new, dec_b, d).transpose(1, 0, 2)
    return (yp, ys, jnp.stack(outs["kp"]), jnp.stack(outs["vp"]), jnp.stack(outs["up"]),
            jnp.stack(outs["ks"]), jnp.stack(outs["vs"]), jnp.stack(outs["us"]))
```

```python
import functools
import math

import jax
import jax.numpy as jnp
from jax import lax
from jax.experimental import pallas as pl
from jax.experimental.pallas import tpu as pltpu

F32 = jnp.float32
BF16 = jnp.bfloat16

EPS = 1e-6
ROPE_THETA = 10000.0
N_HEADS = 8
HEAD_DIM = 128
V_DIM = 2 * HEAD_DIM
D_MODEL = 2048
POOL_WIDTH = 1024
POOL_GROUP = 256
POOL_WINDOWS = (2, 4, 8, 16)
POOL_BUF = 15
LANES = 128
COL_CHUNK = 256
TM_PROJ = 1024
TM_ROWWISE = 512
TN_ADALN = 2048
PAGE_LOOKAHEAD = 4
PAGE_SLOTS = PAGE_LOOKAHEAD + 1
HALO = 16
AUX_WIDTH = 8192
VMEM_LIMIT = 56 * 1024 * 1024


def _params(sem):
    return pltpu.CompilerParams(dimension_semantics=sem, vmem_limit_bytes=VMEM_LIMIT)


def _dot(a, b):
    return jnp.dot(a, b, preferred_element_type=F32)


def _dot_nt(a, b):
    return lax.dot_general(a, b, (((1,), (1,)), ((), ())), preferred_element_type=F32)


def _sigmoid(x):
    return jax.nn.sigmoid(x)


def _mod_kernel(c_ref, w_ref, b_ref, o_ref):
    c = c_ref[...]
    a = (c * _sigmoid(c)).astype(BF16)
    o_ref[...] = _dot(a, w_ref[...].astype(BF16)) + b_ref[...]


def _mod(c_all, w_ada, b_ada, tn=512):
    m, d = c_all.shape
    n = w_ada.shape[1]
    return pl.pallas_call(
        _mod_kernel,
        grid=(n // tn,),
        in_specs=[pl.BlockSpec((m, d), lambda j: (0, 0)),
                  pl.BlockSpec((d, tn), lambda j: (0, j)),
                  pl.BlockSpec((1, tn), lambda j: (0, j))],
        out_specs=pl.BlockSpec((m, tn), lambda j: (0, j)),
        out_shape=jax.ShapeDtypeStruct((m, n), F32),
        compiler_params=_params(("parallel",)),
        name="adaln_mod",
    )(c_all, w_ada, b_ada.reshape(1, n))


def _h_kernel(x_ref, g_ref, sc_ref, sh_ref, o_ref):
    x = x_ref[...]
    ms = jnp.mean(x * x, axis=-1, keepdims=True)
    y = x * lax.rsqrt(ms + EPS) * g_ref[...]
    o_ref[...] = (y * (1.0 + sc_ref[...]) + sh_ref[...]).astype(BF16)


def _modulated_norm(x2d, g_pre, mod, mod_spec, tm):
    m, d = x2d.shape
    return pl.pallas_call(
        _h_kernel,
        grid=(m // tm,),
        in_specs=[pl.BlockSpec((tm, d), lambda i: (i, 0)),
                  pl.BlockSpec((1, d), lambda i: (0, 0)),
                  mod_spec(1), mod_spec(0)],
        out_specs=pl.BlockSpec((tm, d), lambda i: (i, 0)),
        out_shape=jax.ShapeDtypeStruct((m, d), BF16),
        compiler_params=_params(("parallel",)),
        name="modulated_norm",
    )(x2d, g_pre.reshape(1, d), mod, mod)


def _col_chunks(tn):
    return [(c * COL_CHUNK, COL_CHUNK) for c in range(tn // COL_CHUNK)]


def _aux_kernel(h_ref, w_ref, o_ref, *, tn):
    j = pl.program_id(1)
    n_plain = POOL_WIDTH // tn
    n_silu_end = (2 * POOL_WIDTH + D_MODEL) // tn

    def run(epilogue):
        for c0, cw in _col_chunks(tn):
            acc = _dot(h_ref[...], w_ref[:, c0:c0 + cw].astype(BF16))
            o_ref[:, c0:c0 + cw] = epilogue(acc)

    @pl.when(j < n_plain)
    def _():
        run(lambda a: a)

    @pl.when((j >= n_plain) & (j < n_silu_end))
    def _():
        run(lambda a: a * _sigmoid(a))

    @pl.when(j >= n_silu_end)
    def _():
        run(_sigmoid)


def _aux_proj(h, w_in, tm, tn=1024):
    m, d = h.shape
    n_head = (2 * POOL_WIDTH) // tn
    skip = (3 * D_MODEL) // tn
    return pl.pallas_call(
        functools.partial(_aux_kernel, tn=tn),
        grid=(m // tm, AUX_WIDTH // tn),
        in_specs=[pl.BlockSpec((tm, d), lambda i, j: (i, 0)),
                  pl.BlockSpec((d, tn), lambda i, j: (0, jnp.where(j < n_head, j, j + skip)))],
        out_specs=pl.BlockSpec((tm, tn), lambda i, j: (i, j)),
        out_shape=jax.ShapeDtypeStruct((m, AUX_WIDTH), F32),
        compiler_params=_params(("parallel", "arbitrary")),
        name="aux_proj",
    )(h, w_in)


def _rope(acc, cos, sin_signed):
    parts = []
    for c in range(acc.shape[1] // LANES):
        xs = acc[:, c * LANES:(c + 1) * LANES]
        parts.append(xs * cos + pltpu.roll(xs, HEAD_DIM // 2, axis=1) * sin_signed)
    return jnp.concatenate(parts, axis=1)


def _qkv_kernel(h_ref, w_ref, cos_ref, sin_ref, lo_ref, k_ref, v_ref, *, tm, tn):
    j = pl.program_id(1)
    nb = D_MODEL // tn
    n_hm = 2 * N_HEADS

    def chunk(c0, cw):
        return _dot(h_ref[...], w_ref[:, c0:c0 + cw].astype(BF16))

    @pl.when(j < nb)
    def _():
        for c0, cw in _col_chunks(tn):
            r = _rope(chunk(c0, cw), cos_ref[...], sin_ref[...])
            lo_ref[:, c0:c0 + cw] = r.astype(BF16).astype(lo_ref.dtype)

    @pl.when((j >= nb) & (j < 2 * nb))
    def _():
        for c0, cw in _col_chunks(tn):
            r = _rope(chunk(c0, cw), cos_ref[...], sin_ref[...])
            lo_ref[:, c0:c0 + cw] = r.astype(BF16).astype(lo_ref.dtype)
            for e in range(cw // LANES):
                hm = (j - nb) * (tn // LANES) + c0 // LANES + e
                k_ref[pl.ds(hm, tm, stride=n_hm), :] = r[:, e * LANES:(e + 1) * LANES]

    @pl.when(j >= 2 * nb)
    def _():
        for c0, cw in _col_chunks(tn):
            acc = chunk(c0, cw)
            v_ref[:, c0:c0 + cw] = acc
            lo_ref[:, c0:c0 + cw] = acc.astype(BF16).astype(lo_ref.dtype)


def _qkv_proj(h, w_in, cos2, sin2, tm, lo_dtype, tn=512):
    m, d = h.shape
    nb = D_MODEL // tn
    n_hm = 2 * N_HEADS
    col0 = (2 * POOL_WIDTH) // tn
    n_pos = cos2.shape[0] // tm
    return pl.pallas_call(
        functools.partial(_qkv_kernel, tm=tm, tn=tn),
        grid=(m // tm, 3 * nb),
        in_specs=[pl.BlockSpec((tm, d), lambda i, j: (i, 0)),
                  pl.BlockSpec((d, tn), lambda i, j: (0, col0 + j)),
                  pl.BlockSpec((tm, LANES), lambda i, j: (i % n_pos, 0)),
                  pl.BlockSpec((tm, LANES), lambda i, j: (i % n_pos, 0))],
        out_specs=[pl.BlockSpec((tm, tn), lambda i, j: (i, j)),
                   pl.BlockSpec((tm * n_hm, HEAD_DIM), lambda i, j: (i, 0)),
                   pl.BlockSpec((tm, tn), lambda i, j: (i, jnp.clip(j - 2 * nb, 0, nb - 1)))],
        out_shape=[jax.ShapeDtypeStruct((m, 3 * D_MODEL), lo_dtype),
                   jax.ShapeDtypeStruct((m * n_hm, HEAD_DIM), F32),
                   jax.ShapeDtypeStruct((m, D_MODEL), F32)],
        compiler_params=_params(("parallel", "arbitrary")),
        name="qkv_proj",
    )(h, w_in, cos2, sin2)


def _pool_mix(pooled_groups, wg_ref, scale_ref, szp):
    mixed = [_dot(p.astype(BF16), wg_ref[g].astype(BF16)) for g, p in enumerate(pooled_groups)]
    mixed = jnp.concatenate(mixed, axis=1)
    return (mixed * scale_ref[...] * szp).astype(BF16)


def _pool_prompt_kernel(u_ref, halo_ref, szp_ref, wg_ref, scale_ref, o_ref, *, tp, tiles_per_seq):
    i = pl.program_id(0)
    first = (i % tiles_per_seq) == 0
    halo = jnp.where(first, 0.0, halo_ref[...])
    ext = jnp.concatenate([halo, u_ref[...]], axis=0)
    pos = (i % tiles_per_seq) * tp + lax.broadcasted_iota(jnp.int32, (tp, 1), 0)
    groups = []
    for g, w in enumerate(POOL_WINDOWS):
        e = ext[:, g * POOL_GROUP:(g + 1) * POOL_GROUP]
        a = e
        s = 1
        while s < w:
            a = a + pltpu.roll(a, s, axis=0)
            s *= 2
        cnt = jnp.minimum(pos + 1, w).astype(F32)
        groups.append(a[HALO:] / cnt - e[HALO:])
    o_ref[...] = _pool_mix(groups, wg_ref, scale_ref, szp_ref[...])


def _pool_prompt(aux, w_pool_grp, pool_scale, seq, tp=512):
    m = aux.shape[0]
    tiles_per_seq = seq // tp
    hb = tp // HALO
    return pl.pallas_call(
        functools.partial(_pool_prompt_kernel, tp=tp, tiles_per_seq=tiles_per_seq),
        grid=(m // tp,),
        in_specs=[pl.BlockSpec((tp, POOL_WIDTH), lambda i: (i, 0)),
                  pl.BlockSpec((HALO, POOL_WIDTH), lambda i: (jnp.maximum(i * hb - 1, 0), 0)),
                  pl.BlockSpec((tp, POOL_WIDTH), lambda i: (i, 1)),
                  pl.BlockSpec((4, POOL_GROUP, POOL_GROUP), lambda i: (0, 0, 0)),
                  pl.BlockSpec((1, POOL_WIDTH), lambda i: (0, 0))],
        out_specs=pl.BlockSpec((tp, POOL_WIDTH), lambda i: (i, 0)),
        out_shape=jax.ShapeDtypeStruct((m, POOL_WIDTH), BF16),
        compiler_params=_params(("parallel",)),
        name="pool_prompt",
    )(aux, aux, aux, w_pool_grp, pool_scale.reshape(1, POOL_WIDTH))


def _pool_sample_kernel(sp_ref, u_ref, szp_ref, wg_ref, scale_ref, o_ref, *, n_new):
    nb = u_ref.shape[1]
    ext = [sp_ref[r] for r in range(POOL_BUF)] + [u_ref[t] for t in range(n_new)]
    groups = []
    for g, w in enumerate(POOL_WINDOWS):
        per_t = []
        for t in range(n_new):
            end = POOL_BUF + t
            acc = ext[end][:, g * POOL_GROUP:(g + 1) * POOL_GROUP]
            cur = acc
            for r in range(end - w + 1, end):
                acc = acc + ext[r][:, g * POOL_GROUP:(g + 1) * POOL_GROUP]
            per_t.append(acc / float(w) - cur)
        groups.append(jnp.concatenate(per_t, axis=0))
    szp = szp_ref[...].reshape(n_new * nb, POOL_WIDTH)
    o_ref[...] = _pool_mix(groups, wg_ref, scale_ref, szp)


def _pool_sample(sp_t, aux_s3, w_pool_grp, pool_scale):
    n_new, nb, _ = aux_s3.shape
    return pl.pallas_call(
        functools.partial(_pool_sample_kernel, n_new=n_new),
        grid=(1,),
        in_specs=[pl.BlockSpec((POOL_BUF, nb, POOL_WIDTH), lambda i: (0, 0, 0)),
                  pl.BlockSpec((n_new, nb, POOL_WIDTH), lambda i: (0, 0, 0)),
                  pl.BlockSpec((n_new, nb, POOL_WIDTH), lambda i: (0, 0, 1)),
                  pl.BlockSpec((4, POOL_GROUP, POOL_GROUP), lambda i: (0, 0, 0)),
                  pl.BlockSpec((1, POOL_WIDTH), lambda i: (0, 0))],
        out_specs=pl.BlockSpec((n_new * nb, POOL_WIDTH), lambda i: (0, 0)),
        out_shape=jax.ShapeDtypeStruct((n_new * nb, POOL_WIDTH), BF16),
        compiler_params=_params(("arbitrary",)),
        name="pool_sample",
    )(sp_t, aux_s3, aux_s3, w_pool_grp, pool_scale.reshape(1, POOL_WIDTH))


def _lambda(lv, lam_init):
    l1 = jnp.sum(lv[0:1] * lv[1:2], axis=1, keepdims=True)
    l2 = jnp.sum(lv[2:3] * lv[3:4], axis=1, keepdims=True)
    return jnp.exp(l1) - jnp.exp(l2) + lam_init


def _subln_gate(o, g, sz, lam_init):
    ms = jnp.mean(o * o, axis=-1, keepdims=True)
    y = o * lax.rsqrt(ms + EPS) * g
    return (y * (1.0 - lam_init) * sz).astype(BF16)


def _attn_prompt_kernel(lv_ref, q_ref, k_ref, v_ref, sz_ref, g_ref, o_ref, s_ref,
                        *, tq, nq, lam_init, scale):
    qi = pl.program_id(2)
    lam = _lambda(lv_ref[...], lam_init)
    c_exp = scale * math.log2(math.e)

    half = tq // 2

    def fold(x, op):
        r = x[:, :LANES]
        for c in range(1, x.shape[1] // LANES):
            r = op(r, x[:, c * LANES:(c + 1) * LANES])
        return r

    def tile(n):
        row = lax.broadcasted_iota(jnp.int32, (half, half), 0)
        col = lax.broadcasted_iota(jnp.int32, (half, half), 1)
        causal = col <= row
        causal_full = (lax.broadcasted_iota(jnp.int32, (tq, tq), 1)
                       <= lax.broadcasted_iota(jnp.int32, (tq, tq), 0))
        k0 = n * tq
        pv, rinv = [], []
        q = [q_ref[:, mp * HEAD_DIM:(mp + 1) * HEAD_DIM] for mp in range(2)]
        m_run = [None, None]
        for kc in range(n + 1):
            for mp in range(2):
                kblk = k_ref[kc * tq:(kc + 1) * tq, mp * HEAD_DIM:(mp + 1) * HEAD_DIM]
                if kc < n:
                    s = _dot_nt(q[mp], kblk) * c_exp
                    s_ref[mp, :, kc * tq:(kc + 1) * tq] = s
                    f = fold(s, jnp.maximum)
                elif n == 0:
                    s = jnp.where(causal_full, _dot_nt(q[mp], kblk) * c_exp, -jnp.inf)
                    s_ref[mp, :, 0:tq] = s
                    f = fold(s, jnp.maximum)
                else:
                    s_top = jnp.where(causal, _dot_nt(q[mp][:half], kblk[:half]) * c_exp, -jnp.inf)
                    s_bot = _dot_nt(q[mp][half:], kblk) * c_exp
                    s_bot = jnp.concatenate(
                        [s_bot[:, :half], jnp.where(causal, s_bot[:, half:], -jnp.inf)], axis=1)
                    s_ref[mp, :half, k0:k0 + half] = s_top
                    s_ref[mp, half:, k0:k0 + tq] = s_bot
                    f = jnp.concatenate([fold(s_top, jnp.maximum), fold(s_bot, jnp.maximum)], axis=0)
                m_run[mp] = f if m_run[mp] is None else jnp.maximum(m_run[mp], f)
        for mp in range(2):
            mx = jnp.max(m_run[mp], axis=1, keepdims=True)
            l_run = None
            acc = None
            for kc in range(n + 1):
                vblk = v_ref[kc * tq:(kc + 1) * tq, :]
                if kc < n or n == 0:
                    p = jnp.exp2(s_ref[mp, :, kc * tq:(kc + 1) * tq] - mx)
                    f = fold(p, jnp.add)
                    d = _dot(p.astype(BF16), vblk)
                else:
                    p_top = jnp.exp2(s_ref[mp, :half, k0:k0 + half] - mx[:half])
                    p_bot = jnp.exp2(s_ref[mp, half:, k0:k0 + tq] - mx[half:])
                    f = jnp.concatenate([fold(p_top, jnp.add), fold(p_bot, jnp.add)], axis=0)
                    d = jnp.concatenate([_dot(p_top.astype(BF16), vblk[:half]),
                                         _dot(p_bot.astype(BF16), vblk)], axis=0)
                l_run = f if l_run is None else l_run + f
                acc = d if acc is None else acc + d
            rinv.append(1.0 / jnp.sum(l_run, axis=1, keepdims=True))
            pv.append(acc)
        o = pv[0] * rinv[0] - pv[1] * (lam * rinv[1])
        o_ref[...] = _subln_gate(o, g_ref[...], sz_ref[...], lam_init)

    for n in range(nq):
        pl.when(qi == n)(functools.partial(tile, n))


def _attn_prompt(lam_vecs, qkv16, aux, g_subln, batch, seq, lam_init, tq=512):
    m = qkv16.shape[0]
    nq = seq // tq
    sza0 = (2 * POOL_WIDTH) // V_DIM
    kern = functools.partial(_attn_prompt_kernel, tq=tq, nq=nq, lam_init=lam_init, scale=HEAD_DIM ** -0.5)
    return pl.pallas_call(
        kern,
        grid=(batch, N_HEADS, nq),
        in_specs=[pl.BlockSpec((4, HEAD_DIM), lambda b, h, qi: (0, 0)),
                  pl.BlockSpec((tq, V_DIM), lambda b, h, qi: (b * nq + qi, h)),
                  pl.BlockSpec((seq, V_DIM), lambda b, h, qi: (b, N_HEADS + h)),
                  pl.BlockSpec((seq, V_DIM), lambda b, h, qi: (b, 2 * N_HEADS + h)),
                  pl.BlockSpec((tq, V_DIM), lambda b, h, qi: (b * nq + qi, sza0 + h)),
                  pl.BlockSpec((1, V_DIM), lambda b, h, qi: (0, 0))],
        out_specs=pl.BlockSpec((tq, V_DIM), lambda b, h, qi: (b * nq + qi, h)),
        out_shape=jax.ShapeDtypeStruct((m, N_HEADS * V_DIM), BF16),
        scratch_shapes=[pltpu.VMEM((2, tq, seq), F32)],
        compiler_params=_params(("parallel", "parallel", "arbitrary")),
        name="attn_prompt",
    )(lam_vecs, qkv16, qkv16, qkv16, aux, g_subln.reshape(1, V_DIM))


def _attn_decode_kernel(pt_ref, lv_ref, q_ref, kn_ref, vn_ref, sz_ref, g_ref, ck_hbm, cv_hbm, o_ref,
                        kbuf, vbuf, ksem, vsem, s_ref, p_ref, pn_ref, acc_ref,
                        *, pps, nchunk, n_new, page_base, lam_init, scale):
    b = pl.program_id(0)
    j = pl.program_id(1)
    nstep = 2 * nchunk
    g = b * nstep + j
    total = pl.num_programs(0) * nstep
    nrow = n_new * N_HEADS
    ncol = vbuf.shape[2]

    def page_copies(gg, apply):
        gg = jnp.asarray(gg, jnp.int32)
        bb = gg // nstep
        jj = gg % nstep

        @pl.when(jj < nchunk)
        def _():
            slot = (bb * nchunk + jj) % PAGE_SLOTS
            for i in range(pps):
                page = page_base + pt_ref[bb, jj * pps + i]
                apply(pltpu.make_async_copy(ck_hbm.at[page], kbuf.at[slot, i], ksem.at[slot, i]), i)

        @pl.when(jj >= nchunk)
        def _():
            c = jj - nchunk
            slot = (bb * nchunk + c) % PAGE_SLOTS
            for i in range(pps):
                page = page_base + pt_ref[bb, c * pps + i]
                apply(pltpu.make_async_copy(cv_hbm.at[page], vbuf.at[slot, i], vsem.at[slot, i]), i)

    def start(cp, i):
        cp.start(priority=i % 2)

    @pl.when(g == 0)
    def _():
        for d in range(PAGE_LOOKAHEAD):
            page_copies(d, start)

    @pl.when(g + PAGE_LOOKAHEAD < total)
    def _():
        page_copies(g + PAGE_LOOKAHEAD, start)

    page_copies(g, lambda cp, i: cp.wait())
    slot = (b * nchunk + jnp.where(j < nchunk, j, j - nchunk)) % PAGE_SLOTS
    k_refs = [kbuf.at[slot, i] for i in range(pps)]
    v_refs = [vbuf.at[slot, i] for i in range(pps)]

    @pl.when(j < nchunk)
    def _():
        qq = q_ref[...].reshape(nrow, V_DIM)
        rows = lax.broadcasted_iota(jnp.int32, (nrow, ncol), 0)
        cols = lax.broadcasted_iota(jnp.int32, (nrow, ncol), 1)
        valid = (cols & (N_HEADS - 1)) == (rows & (N_HEADS - 1))
        for mp in range(2):
            qm = qq[:, mp * HEAD_DIM:(mp + 1) * HEAD_DIM].astype(BF16)
            for i in range(pps):
                km = k_refs[i][pl.ds(mp, ncol, stride=2), :].astype(BF16)
                s = _dot_nt(qm, km) * scale
                s_ref[j, mp, :, i * ncol:(i + 1) * ncol] = jnp.where(valid, s, -jnp.inf)

    @pl.when(j == nchunk - 1)
    def _():
        lam = _lambda(lv_ref[...], lam_init)
        qq = q_ref[...].reshape(nrow, V_DIM)
        lane = lax.broadcasted_iota(jnp.int32, (nrow, LANES), 1)
        qrow = lax.broadcasted_iota(jnp.int32, (nrow, LANES), 0) >> 3
        sn = [jnp.full((nrow, LANES), -jnp.inf, F32) for _ in range(2)]
        for t in range(n_new):
            ktile = jnp.concatenate([kn_ref[t]] * n_new, axis=0)
            prod = qq * ktile
            for mp in range(2):
                col = jnp.sum(prod[:, mp * HEAD_DIM:(mp + 1) * HEAD_DIM], axis=1, keepdims=True) * scale
                sn[mp] = jnp.where((lane == t) & (qrow >= t), col, sn[mp])
        pn = []
        rinv = []
        for mp in range(2):
            mx = jnp.max(sn[mp], axis=1, keepdims=True)
            for c in range(nchunk):
                mx = jnp.maximum(mx, jnp.max(s_ref[c, mp], axis=1, keepdims=True))
            pnew = jnp.exp(sn[mp] - mx)
            l = jnp.sum(pnew, axis=1, keepdims=True)
            for c in range(nchunk):
                p = jnp.exp(s_ref[c, mp] - mx)
                s_ref[c, mp] = p
                l = l + jnp.sum(p, axis=1, keepdims=True)
            pn.append(pnew)
            rinv.append(1.0 / l)
        r1 = rinv[0]
        r2 = lam * rinv[1]
        for c in range(nchunk):
            p_ref[c] = (s_ref[c, 0] * r1 - s_ref[c, 1] * r2).astype(BF16)
        pn_ref[...] = (pn[0] * r1 - pn[1] * r2).astype(BF16).astype(F32)

    @pl.when(j == nchunk)
    def _():
        acc_ref[...] = jnp.zeros_like(acc_ref)

    @pl.when(j >= nchunk)
    def _():
        c = j - nchunk
        a = None
        for i in range(pps):
            vp = v_refs[i][...].astype(BF16)
            d = _dot(p_ref[c, :, i * ncol:(i + 1) * ncol], vp)
            a = d if a is None else a + d
        acc_ref[...] += a

    @pl.when(j == 2 * nchunk - 1)
    def _():
        o = acc_ref[...]
        pnew = pn_ref[...]
        for t in range(n_new):
            vtile = jnp.concatenate([vn_ref[t]] * n_new, axis=0)
            o = o + pnew[:, t:t + 1] * vtile
        y = _subln_gate(o, g_ref[...], sz_ref[...].reshape(nrow, V_DIM), lam_init)
        o_ref[...] = y.astype(o_ref.dtype).reshape(n_new, N_HEADS, V_DIM)


def _attn_decode(page_table, lam_vecs, qkv_s5, aux_s4, g_subln, ck, cv, page_base, lam_init, pps=4):
    n_new, dec_b = qkv_s5.shape[:2]
    n_pages = page_table.shape[1]
    nchunk = n_pages // pps
    nrow = n_new * N_HEADS
    ncol = cv.shape[1]
    sza0 = (2 * POOL_WIDTH) // (N_HEADS * V_DIM)

    def small(sel):
        return pl.BlockSpec((n_new, None, None, N_HEADS, V_DIM), lambda b, j, pt: (0, b, sel, 0, 0))

    kern = functools.partial(_attn_decode_kernel, pps=pps, nchunk=nchunk, n_new=n_new, page_base=page_base,
                             lam_init=lam_init, scale=HEAD_DIM ** -0.5)
    grid_spec = pltpu.PrefetchScalarGridSpec(
        num_scalar_prefetch=1,
        grid=(dec_b, 2 * nchunk),
        in_specs=[pl.BlockSpec((4, HEAD_DIM), lambda b, j, pt: (0, 0)),
                  small(0), small(1), small(2),
                  pl.BlockSpec((n_new, None, N_HEADS, V_DIM), lambda b, j, pt: (0, b, sza0, 0)),
                  pl.BlockSpec((1, V_DIM), lambda b, j, pt: (0, 0)),
                  pl.BlockSpec(memory_space=pl.ANY),
                  pl.BlockSpec(memory_space=pl.ANY)],
        out_specs=pl.BlockSpec((n_new, None, N_HEADS, V_DIM), lambda b, j, pt: (0, b, 0, 0)),
        scratch_shapes=[pltpu.VMEM((PAGE_SLOTS, pps, 2 * ncol, HEAD_DIM), F32),
                        pltpu.VMEM((PAGE_SLOTS, pps, ncol, V_DIM), F32),
                        pltpu.SemaphoreType.DMA((PAGE_SLOTS, pps)),
                        pltpu.SemaphoreType.DMA((PAGE_SLOTS, pps)),
                        pltpu.VMEM((nchunk, 2, nrow, pps * ncol), F32),
                        pltpu.VMEM((nchunk, nrow, pps * ncol), BF16),
                        pltpu.VMEM((nrow, LANES), F32),
                        pltpu.VMEM((nrow, V_DIM), F32)],
    )
    return pl.pallas_call(
        kern,
        grid_spec=grid_spec,
        out_shape=jax.ShapeDtypeStruct((n_new, dec_b, N_HEADS, V_DIM), F32),
        compiler_params=_params(("arbitrary", "arbitrary")),
        name="attn_decode",
    )(page_table, lam_vecs, qkv_s5, qkv_s5, qkv_s5, aux_s4, g_subln.reshape(1, V_DIM), ck, cv)


def _merge_kernel(po_ref, ao_ref, wpp_ref, wpa_ref, gp_ref, ga_ref, o_ref):
    pp = _dot(po_ref[...].astype(BF16), wpp_ref[...].astype(BF16))
    pa = _dot(ao_ref[...].astype(BF16), wpa_ref[...].astype(BF16))
    o_ref[...] = (gp_ref[...] * pp + ga_ref[...] * pa).astype(BF16)


def _merge(pool_out, attn_out, aux, w_proj_pool, w_proj_attn, tm, tn=512):
    m = pool_out.shape[0]
    gp0 = (2 * POOL_WIDTH + D_MODEL) // tn
    ga0 = gp0 + D_MODEL // tn
    return pl.pallas_call(
        _merge_kernel,
        grid=(m // tm, D_MODEL // tn),
        in_specs=[pl.BlockSpec((tm, POOL_WIDTH), lambda i, j: (i, 0)),
                  pl.BlockSpec((tm, D_MODEL), lambda i, j: (i, 0)),
                  pl.BlockSpec((POOL_WIDTH, tn), lambda i, j: (0, j)),
                  pl.BlockSpec((D_MODEL, tn), lambda i, j: (0, j)),
                  pl.BlockSpec((tm, tn), lambda i, j: (i, gp0 + j)),
                  pl.BlockSpec((tm, tn), lambda i, j: (i, ga0 + j))],
        out_specs=pl.BlockSpec((tm, tn), lambda i, j: (i, j)),
        out_shape=jax.ShapeDtypeStruct((m, D_MODEL), BF16),
        compiler_params=_params(("parallel", "arbitrary")),
        name="gated_merge",
    )(pool_out, attn_out, w_proj_pool, w_proj_attn, aux, aux)


def _out_kernel(m_ref, w_ref, x_ref, gate_ref, g_ref, y_ref, acc_ref, *, tn, nj):
    j = pl.program_id(1)
    acc_ref[j] = _dot(m_ref[...], w_ref[...].astype(BF16))

    @pl.when(j == nj - 1)
    def _():
        ss = None
        for c in range(nj):
            a = acc_ref[c]
            t = jnp.sum(a * a, axis=-1, keepdims=True)
            ss = t if ss is None else ss + t
        r = lax.rsqrt(ss / float(nj * tn) + EPS)
        tm, gate_rows = x_ref.shape[0], gate_ref.shape[0]
        for c in range(nj):
            sl = slice(c * tn, (c + 1) * tn)
            gate = gate_ref[:, sl]
            if gate_rows not in (1, tm):
                gate = jnp.concatenate([gate] * (tm // gate_rows), axis=0)
            y_ref[:, sl] = x_ref[:, sl] + gate * (acc_ref[c] * r * g_ref[:, sl])


def _out_proj(merged, w_out, x2d, mod, mod_spec2, g_post, tm, tn=512):
    m, d = x2d.shape
    nj = d // tn
    return pl.pallas_call(
        functools.partial(_out_kernel, tn=tn, nj=nj),
        grid=(m // tm, nj),
        in_specs=[pl.BlockSpec((tm, d), lambda i, j: (i, 0)),
                  pl.BlockSpec((d, tn), lambda i, j: (0, j)),
                  pl.BlockSpec((tm, d), lambda i, j: (i, 0)),
                  mod_spec2(2),
                  pl.BlockSpec((1, d), lambda i, j: (0, 0))],
        out_specs=pl.BlockSpec((tm, d), lambda i, j: (i, 0)),
        out_shape=jax.ShapeDtypeStruct((m, d), F32),
        scratch_shapes=[pltpu.VMEM((nj, tm, tn), F32)],
        compiler_params=_params(("parallel", "arbitrary")),
        name="out_proj",
    )(merged, w_out, x2d, mod, g_post.reshape(1, d))


def _rope_tables(pos):
    half = HEAD_DIM // 2
    inv_freq = ROPE_THETA ** (-jnp.arange(half, dtype=F32) / half)
    ang = pos.astype(F32)[:, None] * inv_freq[None, :]
    cos = jnp.cos(ang)
    sin = jnp.sin(ang)
    return jnp.concatenate([cos, cos], axis=1), jnp.concatenate([-sin, sin], axis=1)


def kernel(x_prompt, x_sample, cache_k, cache_v, state_pool, page_table, c_prompt, c_sample,
           w_ada, b_ada, g_pre, g_post, w_in, w_pool_grp, pool_scale,
           lambda_q1, lambda_k1, lambda_q2, lambda_k2, g_subln, w_proj_pool, w_proj_attn, w_out):
    batch, seq, d = x_prompt.shape
    dec_b, n_new, _ = x_sample.shape
    depth = w_in.shape[0]
    n_pages = page_table.shape[1]
    n_pool_pages, page_size = cache_k.shape[1], cache_k.shape[2]
    past_len = n_pages * page_size
    width = N_HEADS * V_DIM
    ck = cache_k.reshape(depth * n_pool_pages, page_size * N_HEADS * 2, HEAD_DIM)
    cv = cache_v.reshape(depth * n_pool_pages, page_size * N_HEADS, V_DIM)

    cos_p, sin_p = _rope_tables(jnp.arange(seq))
    cos_s, sin_s = _rope_tables(past_len + jnp.repeat(jnp.arange(n_new), dec_b))

    xp = x_prompt.reshape(batch * seq, d)
    xs = x_sample.transpose(1, 0, 2).reshape(n_new * dec_b, d)
    c_all = jnp.concatenate([c_prompt, c_sample], axis=0)
    tm_p = TM_PROJ

    outs = {name: [] for name in ("kp", "vp", "up", "ks", "vs", "us")}
    for l in range(depth):
        lam_init = 0.8 - 0.6 * math.exp(-0.3 * l)
        lam_vecs = jnp.stack([lambda_q1[l], lambda_k1[l], lambda_q2[l], lambda_k2[l]])
        mod = _mod(c_all, w_ada[l], b_ada[l], TN_ADALN)
        wpp, wpa, wout = (w.astype(BF16) for w in (w_proj_pool[l], w_proj_attn[l], w_out[l]))
        mod_p = mod[:batch].reshape(batch, 1, 3 * d)
        mod_s = mod[batch:]

        def spec_p(tm):
            per_seq = seq // tm
            return lambda col: pl.BlockSpec((None, 1, d), lambda i, *_: (i // per_seq, 0, col))

        def spec_s(col):
            return pl.BlockSpec((dec_b, d), lambda i, *_: (0, col))

        h = _modulated_norm(xp, g_pre[l], mod_p, spec_p(TM_ROWWISE), TM_ROWWISE)
        aux = _aux_proj(h, w_in[l], tm_p)
        qkv16, k32, v32 = _qkv_proj(h, w_in[l], cos_p, sin_p, tm_p, BF16)
        pool_out = _pool_prompt(aux, w_pool_grp[l], pool_scale[l], seq)
        attn_out = _attn_prompt(lam_vecs, qkv16, aux, g_subln[l], batch, seq, lam_init)
        merged = _merge(pool_out, attn_out, aux, wpp, wpa, tm_p)
        xp_new = _out_proj(merged, wout, xp, mod_p, spec_p(TM_ROWWISE), g_post[l], TM_ROWWISE)
        outs["kp"].append(k32.reshape(batch, seq, N_HEADS, 2, HEAD_DIM))
        outs["vp"].append(v32.reshape(batch, seq, N_HEADS, V_DIM))
        outs["up"].append(aux.reshape(batch, seq, AUX_WIDTH)[:, seq - POOL_BUF:, :POOL_WIDTH])

        ms = n_new * dec_b
        h_s = _modulated_norm(xs, g_pre[l], mod_s, spec_s, dec_b)
        aux_s = _aux_proj(h_s, w_in[l], ms)
        qkv_s, k32_s, v32_s = _qkv_proj(h_s, w_in[l], cos_s, sin_s, ms, F32)
        sp_t = state_pool[l].transpose(1, 0, 2)
        pool_out_s = _pool_sample(sp_t, aux_s.reshape(n_new, dec_b, AUX_WIDTH), w_pool_grp[l], pool_scale[l])
        attn_out_s = _attn_decode(
            page_table, lam_vecs,
            qkv_s.reshape(n_new, dec_b, 3, N_HEADS, V_DIM),
            aux_s.reshape(n_new, dec_b, AUX_WIDTH // V_DIM, V_DIM),
            g_subln[l], ck, cv, l * n_pool_pages, lam_init)
        merged_s = _merge(pool_out_s, attn_out_s.reshape(ms, width), aux_s, wpp, wpa, ms)
        xs_new = _out_proj(merged_s, wout, xs, mod_s, spec_s, g_post[l], ms)

        def to_bt(a):
            return a.reshape(n_new, dec_b, -1).transpose(1, 0, 2)

        outs["ks"].append(to_bt(k32_s).reshape(dec_b, n_new, N_HEADS, 2, HEAD_DIM))
        outs["vs"].append(to_bt(v32_s).reshape(dec_b, n_new, N_HEADS, V_DIM))
        u_ext = jnp.concatenate([state_pool[l], to_bt(aux_s[:, :POOL_WIDTH])], axis=1)
        outs["us"].append(u_ext[:, -POOL_BUF:])
        xp, xs = xp_new, xs_new

    yp = xp.reshape(batch, seq, d)
    ys = xs.reshape(n_new, dec_b, d).transpose(1, 0, 2)
    return (yp, ys, jnp.stack(outs["kp"]), jnp.stack(outs["vp"]), jnp.stack(outs["up"]),
            jnp.stack(outs["ks"]), jnp.stack(outs["vs"]), jnp.stack(outs["us"]))
```
